```python
import jax, jax.numpy as jnp
from jax import lax
import numpy as np

D_MODEL = 1024
BATCH = 32
SEQ = 2048
DEPTH = 2

GRID_W = 64
Q_BLOCK = 128
ROPE_THETA = 10000.0
NORM_EPS = 1e-6
MLA_HEADS = 8
MLA_Q_RANK = 384
MLA_KV_RANK = 256
MLA_NOPE_DIM = 64
MLA_ROPE_DIM = 32
MLA_V_DIM = 64
MLA_QK_DIM = MLA_NOPE_DIM + MLA_ROPE_DIM
MLA_OUT = MLA_HEADS * MLA_V_DIM
GQA_Q_HEADS = 8
GQA_KV_HEADS = 2
GQA_GROUP = GQA_Q_HEADS // GQA_KV_HEADS
GQA_HEAD_DIM = 64
GQA_OUT = GQA_Q_HEADS * GQA_HEAD_DIM
N_BRANCHES = 2
D_FF = 2816
N_MOD = 9
IN_SIZES = (MLA_Q_RANK, MLA_KV_RANK, MLA_ROPE_DIM,
            GQA_Q_HEADS * GQA_HEAD_DIM, GQA_KV_HEADS * GQA_HEAD_DIM, GQA_KV_HEADS * GQA_HEAD_DIM,
            N_BRANCHES * D_MODEL)
D_IN = sum(IN_SIZES)

kernel_name = 'hybrid_mla_gqa_axial_macaron_adaln_encoder'


def _rmsnorm(x, g):
    x32 = x.astype(jnp.float32)
    y = x32 * lax.rsqrt(jnp.mean(x32 * x32, axis=-1, keepdims=True) + NORM_EPS)
    return (y * g.astype(jnp.float32)).astype(x.dtype)


def _split_cols(z, sizes):
    out, start = [], 0
    for s in sizes:
        out.append(z[..., start:start + s])
        start += s
    return out


def _rope_tables(pos, dim):
    inv = ROPE_THETA ** (-jnp.arange(0, dim, 2, dtype=jnp.float32) / dim)
    ang = pos.astype(jnp.float32)[:, None] * inv[None, :]
    return jnp.cos(ang), jnp.sin(ang)


def _apply_rope(x, cos, sin):
    x32 = x.astype(jnp.float32)
    x1, x2 = jnp.split(x32, 2, axis=-1)
    c = cos[None, :, None, :]
    s = sin[None, :, None, :]
    return jnp.concatenate([x1 * c - x2 * s, x2 * c + x1 * s], axis=-1).astype(x.dtype)


def _axial_rope(x, row_cs, col_cs):
    half = x.shape[-1] // 2
    return jnp.concatenate([_apply_rope(x[..., :half], *row_cs),
                            _apply_rope(x[..., half:], *col_cs)], axis=-1)


def _block_attention(q, k, v, scale):
    B, S, Hk, G, Dq = q.shape
    nb = S // Q_BLOCK
    qb = q.reshape(B, nb, Q_BLOCK, Hk, G, Dq).transpose(1, 0, 2, 3, 4, 5)

    def one_block(q_blk):
        s = jnp.einsum('bqhgd,bshd->bhgqs', q_blk, k, preferred_element_type=jnp.float32) * scale
        p = jax.nn.softmax(s, axis=-1).astype(v.dtype)
        return jnp.einsum('bhgqs,bshd->bqhgd', p, v)

    o = lax.map(one_block, qb)
    return o.transpose(1, 0, 2, 3, 4, 5).reshape(B, S, Hk * G * v.shape[-1])


def _swiglu(x, w_in, w_out):
    a, b = jnp.split(x @ w_in, 2, axis=-1)
    return (jax.nn.silu(a) * b) @ w_out


def _modulate(xn, shift, scale):
    return xn * (1.0 + scale) + shift


def _mla(z_q, z_kv, z_kr, q_a_norm, w_uq, kv_a_norm, w_ukv, qk_q_norm, qk_k_norm, pos_cs):
    B, S, _ = z_q.shape
    q = (_rmsnorm(z_q, q_a_norm) @ w_uq).reshape(B, S, MLA_HEADS, MLA_QK_DIM)
    kv = (_rmsnorm(z_kv, kv_a_norm) @ w_ukv).reshape(B, S, MLA_HEADS, MLA_NOPE_DIM + MLA_V_DIM)
    k_nope, v = kv[..., :MLA_NOPE_DIM], kv[..., MLA_NOPE_DIM:]
    k_pe = jnp.broadcast_to(z_kr[:, :, None, :], (B, S, MLA_HEADS, MLA_ROPE_DIM))
    k = jnp.concatenate([k_nope, k_pe], axis=-1)
    q = _rmsnorm(q, qk_q_norm)
    k = _rmsnorm(k, qk_k_norm)
    q = jnp.concatenate([q[..., :MLA_NOPE_DIM], _apply_rope(q[..., MLA_NOPE_DIM:], *pos_cs)], axis=-1)
    k = jnp.concatenate([k[..., :MLA_NOPE_DIM], _apply_rope(k[..., MLA_NOPE_DIM:], *pos_cs)], axis=-1)
    return _block_attention(q[:, :, :, None, :], k, v, MLA_QK_DIM ** -0.5)


def _gqa_axial(z_q, z_k, z_v, q_norm, k_norm, row_cs, col_cs):
    B, S, _ = z_q.shape
    q = _rmsnorm(z_q.reshape(B, S, GQA_Q_HEADS, GQA_HEAD_DIM), q_norm)
    k = _rmsnorm(z_k.reshape(B, S, GQA_KV_HEADS, GQA_HEAD_DIM), k_norm)
    v = z_v.reshape(B, S, GQA_KV_HEADS, GQA_HEAD_DIM)
    q = _axial_rope(q, row_cs, col_cs).reshape(B, S, GQA_KV_HEADS, GQA_GROUP, GQA_HEAD_DIM)
    k = _axial_rope(k, row_cs, col_cs)
    return _block_attention(q, k, v, GQA_HEAD_DIM ** -0.5)


def setup_inputs(seed: int = 0) -> dict:
    key = jax.random.key(seed)
    ks = jax.random.split(key, 24)
    f32 = jnp.float32

    def w(k, shape, fan_in):
        return jax.random.normal(k, shape, f32) * (fan_in ** -0.5)

    def gain(k, shape):
        return 1.0 + 0.02 * jax.random.normal(k, shape, f32)

    L = DEPTH
    return {
        'x': jax.random.normal(ks[0], (BATCH, SEQ, D_MODEL), f32),
        'c': jax.random.normal(ks[1], (BATCH, D_MODEL), f32),
        'w_ada': w(ks[2], (L, D_MODEL, N_MOD * D_MODEL), D_MODEL),
        'b_ada': 0.02 * jax.random.normal(ks[3], (L, N_MOD * D_MODEL), f32),
        'norm_ffn1': gain(ks[4], (L, D_MODEL)),
        'w_ffn1_in': w(ks[5], (L, D_MODEL, 2 * D_FF), D_MODEL),
        'w_ffn1_out': w(ks[6], (L, D_FF, D_MODEL), D_FF),
        'norm_mix': gain(ks[7], (L, D_MODEL)),
        'w_in': w(ks[8], (L, D_MODEL, D_IN), D_MODEL),
        'mla_q_a_norm': gain(ks[9], (L, MLA_Q_RANK)),
        'mla_w_uq': w(ks[10], (L, MLA_Q_RANK, MLA_HEADS * MLA_QK_DIM), MLA_Q_RANK),
        'mla_kv_a_norm': gain(ks[11], (L, MLA_KV_RANK)),
        'mla_w_ukv': w(ks[12], (L, MLA_KV_RANK, MLA_HEADS * (MLA_NOPE_DIM + MLA_V_DIM)), MLA_KV_RANK),
        'mla_qk_q_norm': gain(ks[13], (L, MLA_QK_DIM)),
        'mla_qk_k_norm': gain(ks[14], (L, MLA_QK_DIM)),
        'gqa_q_norm': gain(ks[15], (L, GQA_HEAD_DIM)),
        'gqa_k_norm': gain(ks[16], (L, GQA_HEAD_DIM)),
        'w_branch_mla': w(ks[17], (L, MLA_OUT, D_MODEL), MLA_OUT),
        'w_branch_gqa': w(ks[18], (L, GQA_OUT, D_MODEL), GQA_OUT),
        'w_out': w(ks[19], (L, D_MODEL, D_MODEL), D_MODEL),
        'norm_ffn2': gain(ks[20], (L, D_MODEL)),
        'w_ffn2_in': w(ks[21], (L, D_MODEL, 2 * D_FF), D_MODEL),
        'w_ffn2_out': w(ks[22], (L, D_FF, D_MODEL), D_FF),
    }


def reference(x, c, w_ada, b_ada, norm_ffn1, w_ffn1_in, w_ffn1_out, norm_mix, w_in,
              mla_q_a_norm, mla_w_uq, mla_kv_a_norm, mla_w_ukv, mla_qk_q_norm, mla_qk_k_norm,
              gqa_q_norm, gqa_k_norm, w_branch_mla, w_branch_gqa, w_out,
              norm_ffn2, w_ffn2_in, w_ffn2_out):
    B, S, D = x.shape
    rows = S // GRID_W
    t = jnp.arange(S)
    row = jnp.repeat(jnp.arange(rows), GRID_W)
    col = jnp.tile(jnp.arange(GRID_W), rows)
    pos_cs = _rope_tables(t, MLA_ROPE_DIM)
    row_cs = _rope_tables(row, GQA_HEAD_DIM // 2)
    col_cs = _rope_tables(col, GQA_HEAD_DIM // 2)
    c_act = jax.nn.silu(c)

    h = x
    for l in range(DEPTH):
        mod = (c_act @ w_ada[l] + b_ada[l]).reshape(B, N_MOD, 1, D)
        sh1, sc1, g1, sh2, sc2, g2, sh3, sc3, g3 = [mod[:, i] for i in range(N_MOD)]

        u = _modulate(_rmsnorm(h, norm_ffn1[l]), sh1, sc1)
        h = h + 0.5 * g1 * _swiglu(u, w_ffn1_in[l], w_ffn1_out[l])

        u = _modulate(_rmsnorm(h, norm_mix[l]), sh2, sc2)
        z_q, z_kv, z_kr, z_gq, z_gk, z_gv, z_gate = _split_cols(u @ w_in[l], IN_SIZES)
        o_mla = _mla(z_q, z_kv, z_kr, mla_q_a_norm[l], mla_w_uq[l], mla_kv_a_norm[l], mla_w_ukv[l],
                     mla_qk_q_norm[l], mla_qk_k_norm[l], pos_cs)
        o_gqa = _gqa_axial(z_gq, z_gk, z_gv, gqa_q_norm[l], gqa_k_norm[l], row_cs, col_cs)
        gate_mla, gate_gqa = jnp.split(jax.nn.sigmoid(z_gate), N_BRANCHES, axis=-1)
        merged = gate_mla * (o_mla @ w_branch_mla[l]) + gate_gqa * (o_gqa @ w_branch_gqa[l])
        h = h + g2 * (merged @ w_out[l])

        u = _modulate(_rmsnorm(h, norm_ffn2[l]), sh3, sc3)
        h = h + 0.5 * g3 * _swiglu(u, w_ffn2_in[l], w_ffn2_out[l])
    return h
```

```python
import functools
import math

import jax
import jax.numpy as jnp
from jax import lax
from jax.experimental import pallas as pl
from jax.experimental.pallas import tpu as pltpu

D_MODEL = 1024
DEPTH = 2
GRID_W = 64
ROPE_THETA = 10000.0
NORM_EPS = 1e-6
MLA_HEADS = 8
MLA_Q_RANK = 384
MLA_KV_RANK = 256
MLA_NOPE_DIM = 64
MLA_ROPE_DIM = 32
MLA_V_DIM = 64
MLA_QK_DIM = MLA_NOPE_DIM + MLA_ROPE_DIM
GQA_Q_HEADS = 8
GQA_KV_HEADS = 2
GQA_GROUP = GQA_Q_HEADS // GQA_KV_HEADS
GQA_HEAD_DIM = 64
D_FF = 2816
N_MOD = 9

LANES = 128
HALF_ROT = 16
LOG2E = math.log2(math.e)

_C_Q = 0
_C_KR = _C_Q + MLA_Q_RANK
_C_KV = _C_KR + LANES
_C_GQ = _C_KV + MLA_KV_RANK
_C_GK = _C_GQ + GQA_Q_HEADS * LANES
_C_GATE = _C_GK + GQA_KV_HEADS * LANES
_C_GV = _C_GATE + 2 * D_MODEL
_C_END = _C_GV + GQA_KV_HEADS * GQA_HEAD_DIM

VMEM_LIMIT = 56 * 1024 * 1024

FFN_TM = 1024
FFN_TSUB = 256
FFN_TF = 256
MIX_TM = 512
ATT_TQ_MLA = 256
ATT_TQ_GQA = 128
OUT_TM = 512

_BF = jnp.bfloat16
_F32 = jnp.float32


def _params():
    return pltpu.CompilerParams(dimension_semantics=("arbitrary", "arbitrary"),
                                vmem_limit_bytes=VMEM_LIMIT)


def _dot(a, b):
    return jnp.dot(a, b, preferred_element_type=_F32)


def _dot_nt(a, b):
    return lax.dot_general(a, b, (((1,), (1,)), ((), ())), preferred_element_type=_F32)


def _resident(block_shape, index_map):
    return pl.BlockSpec(block_shape, index_map, pipeline_mode=pl.Buffered(1))


def _rms(x, inv_n):
    return x * lax.rsqrt(jnp.sum(x * x, axis=-1, keepdims=True) * inv_n + NORM_EPS)


def _ada_kernel(c_ref, w_ref, b_ref, o_ref):
    c = c_ref[...]
    c_act = c / (1.0 + jnp.exp(-c))
    o_ref[0] = _dot(c_act.astype(_BF), w_ref[0].astype(_BF)) + b_ref[0]


def _ada(c, w_ada, b_ada):
    L, D, N = w_ada.shape
    B = c.shape[0]
    tn = D_MODEL
    return pl.pallas_call(
        _ada_kernel,
        out_shape=jax.ShapeDtypeStruct((L, B, N), _F32),
        grid=(L, N // tn),
        in_specs=[
            pl.BlockSpec((B, D), lambda l, j: (0, 0)),
            pl.BlockSpec((1, D, tn), lambda l, j: (l, 0, j)),
            pl.BlockSpec((1, 1, tn), lambda l, j: (l, 0, j)),
        ],
        out_specs=pl.BlockSpec((1, B, tn), lambda l, j: (l, 0, j)),
        compiler_params=_params(),
        name="ada",
    )(c, w_ada, b_ada.reshape(L, 1, N))


def _ffn_kernel(h_ref, mod_ref, g_ref, win_ref, wout_ref, o_ref, act_ref, *, mod0):
    shift = mod_ref[0, 0, mod0:mod0 + 1, :]
    scale1 = (1.0 + mod_ref[0, 0, mod0 + 1:mod0 + 2, :]) * g_ref[...]
    half_gate = 0.5 * mod_ref[0, 0, mod0 + 2:mod0 + 3, :]

    def sub(r, carry):
        r0 = pl.multiple_of(r * FFN_TSUB, FFN_TSUB)
        x = h_ref[0, pl.ds(r0, FFN_TSUB), :]
        u = (_rms(x, 1.0 / D_MODEL) * scale1 + shift).astype(_BF)
        for c0 in range(0, D_FF, FFN_TF):
            a = _dot(u, win_ref[:, c0:c0 + FFN_TF])
            b = _dot(u, win_ref[:, D_FF + c0:D_FF + c0 + FFN_TF])
            act_ref[:, c0:c0 + FFN_TF] = (a / (1.0 + jnp.exp(-a)) * b).astype(_BF)
        y = _dot(act_ref[...], wout_ref[...])
        o_ref[0, pl.ds(r0, FFN_TSUB), :] = x + half_gate * y
        return carry

    lax.fori_loop(0, FFN_TM // FFN_TSUB, sub, 0)


def _ffn(h, mod, gain, w_in, w_out, *, layer, mod0):
    B, S, D = h.shape
    kern = functools.partial(_ffn_kernel, mod0=mod0)
    return pl.pallas_call(
        kern,
        out_shape=jax.ShapeDtypeStruct((B, S, D), _F32),
        grid=(B, S // FFN_TM),
        in_specs=[
            pl.BlockSpec((1, FFN_TM, D), lambda b, i: (b, i, 0)),
            pl.BlockSpec((1, 1, N_MOD, D), lambda b, i: (layer, b, 0, 0)),
            pl.BlockSpec((None, 1, D), lambda b, i: (layer, 0, 0)),
            _resident((None, D, 2 * D_FF), lambda b, i: (layer, 0, 0)),
            _resident((None, D_FF, D), lambda b, i: (layer, 0, 0)),
        ],
        out_specs=pl.BlockSpec((1, FFN_TM, D), lambda b, i: (b, i, 0)),
        scratch_shapes=[pltpu.VMEM((FFN_TSUB, D_FF), _BF)],
        compiler_params=_params(),
        name="ffn",
    )(h, mod, gain, w_in, w_out)


def _rope(x, cos, sin_a, sin_b):
    fwd = pltpu.roll(x, LANES - HALF_ROT, 1)
    bwd = pltpu.roll(x, HALF_ROT, 1)
    return x * cos + fwd * sin_a + bwd * sin_b


def _mix_in_kernel(h_ref, mod_ref, g_ref, win_ref, qan_ref, wuq_ref, kvan_ref, wuk_ref, wuv_ref,
                   qn_ref, kn_ref, gqn_ref, gkn_ref, rm_ref, rg_ref,
                   qm_ref, km_ref, vm_ref, qg_ref, kg_ref, vg_ref, gate_ref):
    shift = mod_ref[0, 0, 3:4, :]
    scale1 = (1.0 + mod_ref[0, 0, 4:5, :]) * g_ref[...]
    u = (_rms(h_ref[0], 1.0 / D_MODEL) * scale1 + shift).astype(_BF)

    def proj(c0, c1):
        return _dot(u, win_ref[:, c0:c1])

    cos_m, sa_m, sb_m = rm_ref[0], rm_ref[1], rm_ref[2]
    cos_g, sa_g, sb_g = rg_ref[0], rg_ref[1], rg_ref[2]

    zq = proj(_C_Q, _C_KV)
    q_lat = (_rms(zq[:, :MLA_Q_RANK], 1.0 / MLA_Q_RANK) * qan_ref[...]).astype(_BF)
    k_pe = zq[:, MLA_Q_RANK:]
    q_up = _dot(q_lat, wuq_ref[...])
    q_gain = qn_ref[...] * (MLA_QK_DIM ** -0.5 * LOG2E)
    for hd in range(MLA_HEADS):
        qh = _rms(q_up[:, hd * LANES:(hd + 1) * LANES], 1.0 / MLA_QK_DIM) * q_gain
        qm_ref[0, hd] = _rope(qh, cos_m, sa_m, sb_m).astype(_BF)

    zkv = proj(_C_KV, _C_GQ)
    kv_lat = (_rms(zkv, 1.0 / MLA_KV_RANK) * kvan_ref[...]).astype(_BF)
    k_up = _dot(kv_lat, wuk_ref[...])
    for hd in range(MLA_HEADS):
        kh = _rms(k_up[:, hd * LANES:(hd + 1) * LANES] + k_pe, 1.0 / MLA_QK_DIM) * kn_ref[...]
        km_ref[0, hd] = _rope(kh, cos_m, sa_m, sb_m).astype(_BF)
    vm_ref[0] = _dot(kv_lat, wuv_ref[...]).T.astype(_BF)

    gq = proj(_C_GQ, _C_GK)
    gq_gain = gqn_ref[...] * (GQA_HEAD_DIM ** -0.5 * LOG2E)
    for hd in range(GQA_Q_HEADS):
        qh = _rms(gq[:, hd * LANES:(hd + 1) * LANES], 1.0 / GQA_HEAD_DIM) * gq_gain
        qg_ref[0, hd] = _rope(qh, cos_g, sa_g, sb_g).astype(_BF)
    gk = proj(_C_GK, _C_GATE)
    for hd in range(GQA_KV_HEADS):
        kh = _rms(gk[:, hd * LANES:(hd + 1) * LANES], 1.0 / GQA_HEAD_DIM) * gkn_ref[...]
        kg_ref[0, hd] = _rope(kh, cos_g, sa_g, sb_g).astype(_BF)
    vg_ref[0] = proj(_C_GV, _C_END).T.astype(_BF)

    zg = proj(_C_GATE, _C_GV)
    gate_ref[0] = (1.0 / (1.0 + jnp.exp(-zg))).astype(_BF)


def _mix_in(h, mod, gain, w, rope_m, rope_g, *, layer):
    B, S, D = h.shape
    tm = MIX_TM
    full = lambda shape: _resident(shape, lambda b, i: (0,) * len(shape))
    head_out = lambda nh: pl.BlockSpec((1, nh, tm, LANES), lambda b, i: (b, 0, i, 0))
    out_shape = (
        jax.ShapeDtypeStruct((B, MLA_HEADS, S, LANES), _BF),
        jax.ShapeDtypeStruct((B, MLA_HEADS, S, LANES), _BF),
        jax.ShapeDtypeStruct((B, MLA_HEADS * MLA_V_DIM, S), _BF),
        jax.ShapeDtypeStruct((B, GQA_Q_HEADS, S, LANES), _BF),
        jax.ShapeDtypeStruct((B, GQA_KV_HEADS, S, LANES), _BF),
        jax.ShapeDtypeStruct((B, GQA_KV_HEADS * GQA_HEAD_DIM, S), _BF),
        jax.ShapeDtypeStruct((B, S, 2 * D), _BF),
    )
    out_specs = (
        head_out(MLA_HEADS), head_out(MLA_HEADS),
        pl.BlockSpec((1, MLA_HEADS * MLA_V_DIM, tm), lambda b, i: (b, 0, i)),
        head_out(GQA_Q_HEADS), head_out(GQA_KV_HEADS),
        pl.BlockSpec((1, GQA_KV_HEADS * GQA_HEAD_DIM, tm), lambda b, i: (b, 0, i)),
        pl.BlockSpec((1, tm, 2 * D), lambda b, i: (b, i, 0)),
    )
    in_specs = [
        pl.BlockSpec((1, tm, D), lambda b, i: (b, i, 0)),
        pl.BlockSpec((1, 1, N_MOD, D), lambda b, i: (layer, b, 0, 0)),
        pl.BlockSpec((None, 1, D), lambda b, i: (layer, 0, 0)),
        full(w["w_in"].shape), full(w["q_a_norm"].shape), full(w["w_uq"].shape),
        full(w["kv_a_norm"].shape), full(w["w_uk"].shape), full(w["w_uv"].shape),
        full(w["qk_q_norm"].shape), full(w["qk_k_norm"].shape),
        full(w["gqa_q_norm"].shape), full(w["gqa_k_norm"].shape),
        pl.BlockSpec((3, tm, LANES), lambda b, i: (0, i, 0)),
        pl.BlockSpec((3, tm, LANES), lambda b, i: (0, i, 0)),
    ]
    return pl.pallas_call(
        _mix_in_kernel,
        out_shape=out_shape,
        grid=(B, S // tm),
        in_specs=in_specs,
        out_specs=out_specs,
        compiler_params=_params(),
        name="mix_in",
    )(h, mod, gain, w["w_in"], w["q_a_norm"], w["w_uq"], w["kv_a_norm"], w["w_uk"], w["w_uv"],
      w["qk_q_norm"], w["qk_k_norm"], w["gqa_q_norm"], w["gqa_k_norm"], rope_m, rope_g)


def _softmax_pv(s_t, v_t):
    m = jnp.max(s_t, axis=0, keepdims=True)
    e = jnp.exp2(s_t - m)
    denom = jnp.sum(e, axis=0, keepdims=True)
    return _dot(v_t, e.astype(_BF)) / denom


def _attn_mla_kernel(q_ref, k_ref, v_ref, o_ref, ot_ref):
    def head(hd, carry):
        s_t = _dot_nt(k_ref[0, hd], q_ref[0, hd])
        r0 = pl.multiple_of(hd * MLA_V_DIM, MLA_V_DIM)
        ot_ref[pl.ds(r0, MLA_V_DIM), :] = _softmax_pv(s_t, v_ref[0, pl.ds(r0, MLA_V_DIM), :])
        return carry

    lax.fori_loop(0, MLA_HEADS, head, 0)
    o_ref[0] = ot_ref[...].T.astype(_BF)


def _attn_mla(q, k, v_t):
    B, H, S, _ = q.shape
    tq = ATT_TQ_MLA
    return pl.pallas_call(
        _attn_mla_kernel,
        out_shape=jax.ShapeDtypeStruct((B, S, H * MLA_V_DIM), _BF),
        grid=(B, S // tq),
        in_specs=[
            pl.BlockSpec((1, H, tq, LANES), lambda b, i: (b, 0, i, 0)),
            pl.BlockSpec((1, H, S, LANES), lambda b, i: (b, 0, 0, 0)),
            pl.BlockSpec((1, H * MLA_V_DIM, S), lambda b, i: (b, 0, 0)),
        ],
        out_specs=pl.BlockSpec((1, tq, H * MLA_V_DIM), lambda b, i: (b, i, 0)),
        scratch_shapes=[pltpu.VMEM((H * MLA_V_DIM, tq), _F32)],
        compiler_params=_params(),
        name="attn_mla",
    )(q, k, v_t)


def _attn_gqa_kernel(q_ref, k_ref, v_ref, o_ref, ot_ref):
    tq = ATT_TQ_GQA
    for kv in range(GQA_KV_HEADS):
        q_grp = jnp.concatenate([q_ref[0, kv * GQA_GROUP + g] for g in range(GQA_GROUP)], axis=0)
        s_t = _dot_nt(k_ref[0, kv], q_grp)
        o_t = _softmax_pv(s_t, v_ref[0, kv * GQA_HEAD_DIM:(kv + 1) * GQA_HEAD_DIM, :])
        for g in range(GQA_GROUP):
            r0 = (kv * GQA_GROUP + g) * GQA_HEAD_DIM
            ot_ref[r0:r0 + GQA_HEAD_DIM, :] = o_t[:, g * tq:(g + 1) * tq]
    o_ref[0] = ot_ref[...].T.astype(_BF)


def _attn_gqa(q, k, v_t):
    B, H, S, _ = q.shape
    tq = ATT_TQ_GQA
    return pl.pallas_call(
        _attn_gqa_kernel,
        out_shape=jax.ShapeDtypeStruct((B, S, H * GQA_HEAD_DIM), _BF),
        grid=(B, S // tq),
        in_specs=[
            pl.BlockSpec((1, H, tq, LANES), lambda b, i: (b, 0, i, 0)),
            pl.BlockSpec((1, GQA_KV_HEADS, S, LANES), lambda b, i: (b, 0, 0, 0)),
            pl.BlockSpec((1, GQA_KV_HEADS * GQA_HEAD_DIM, S), lambda b, i: (b, 0, 0)),
        ],
        out_specs=pl.BlockSpec((1, tq, H * GQA_HEAD_DIM), lambda b, i: (b, i, 0)),
        scratch_shapes=[pltpu.VMEM((H * GQA_HEAD_DIM, tq), _F32)],
        compiler_params=_params(),
        name="attn_gqa",
    )(q, k, v_t)


def _mix_out_kernel(h_ref, mod_ref, om_ref, og_ref, gate_ref, wbm_ref, wbg_ref, wo_ref, o_ref):
    g2 = mod_ref[0, 0, 5:6, :]
    bm = _dot(om_ref[0], wbm_ref[...])
    bg = _dot(og_ref[0], wbg_ref[...])
    merged = gate_ref[0, :, :D_MODEL] * bm + gate_ref[0, :, D_MODEL:] * bg
    o_ref[0] = h_ref[0] + g2 * _dot(merged.astype(_BF), wo_ref[...])


def _mix_out(h, mod, o_mla, o_gqa, gate, wb_mla, wb_gqa, w_out, *, layer):
    B, S, D = h.shape
    tm = OUT_TM
    row = lambda n: pl.BlockSpec((1, tm, n), lambda b, i: (b, i, 0))
    wspec = lambda a: _resident((None,) + a.shape[1:], lambda b, i: (layer, 0, 0))
    return pl.pallas_call(
        _mix_out_kernel,
        out_shape=jax.ShapeDtypeStruct((B, S, D), _F32),
        grid=(B, S // tm),
        in_specs=[
            row(D),
            pl.BlockSpec((1, 1, N_MOD, D), lambda b, i: (layer, b, 0, 0)),
            row(o_mla.shape[-1]), row(o_gqa.shape[-1]), row(2 * D),
            wspec(wb_mla), wspec(wb_gqa), wspec(w_out),
        ],
        out_specs=row(D),
        compiler_params=_params(),
        name="mix_out",
    )(h, mod, o_mla, o_gqa, gate, wb_mla, wb_gqa, w_out)


def _rope_angles(pos, dim):
    inv = ROPE_THETA ** (-jnp.arange(0, dim, 2, dtype=_F32) / dim)
    ang = pos.astype(_F32)[:, None] * inv[None, :]
    return jnp.cos(ang), jnp.sin(ang)


def _rope_tables(S):
    t = jnp.arange(S)
    one = lambda n: jnp.ones((S, n), _F32)
    zero = lambda n: jnp.zeros((S, n), _F32)
    c, s = _rope_angles(t, MLA_ROPE_DIM)
    rope_m = jnp.stack([
        jnp.concatenate([one(MLA_NOPE_DIM), c, c, one(LANES - MLA_QK_DIM)], axis=1),
        jnp.concatenate([zero(MLA_NOPE_DIM), -s, zero(HALF_ROT), zero(LANES - MLA_QK_DIM)], axis=1),
        jnp.concatenate([zero(MLA_NOPE_DIM), zero(HALF_ROT), s, zero(LANES - MLA_QK_DIM)], axis=1),
    ])
    cr, sr = _rope_angles(t // GRID_W, GQA_HEAD_DIM // 2)
    cc, sc = _rope_angles(t % GRID_W, GQA_HEAD_DIM // 2)
    pad = LANES - GQA_HEAD_DIM
    rope_g = jnp.stack([
        jnp.concatenate([cr, cr, cc, cc, one(pad)], axis=1),
        jnp.concatenate([-sr, zero(HALF_ROT), -sc, zero(HALF_ROT), zero(pad)], axis=1),
        jnp.concatenate([zero(HALF_ROT), sr, zero(HALF_ROT), sc, zero(pad)], axis=1),
    ])
    return rope_m, rope_g


def _pad_heads(w, heads, dim):
    lead = w.shape[:-1]
    w = w.reshape(lead + (heads, dim))
    w = jnp.pad(w, [(0, 0)] * len(lead) + [(0, 0), (0, LANES - dim)])
    return w.reshape(lead + (heads * LANES,))


def _pack_mix_weights(l, w_in, q_a_norm, w_uq, kv_a_norm, w_ukv, qk_q_norm, qk_k_norm, gqa_q_norm, gqa_k_norm):
    D = D_MODEL
    wi = w_in[l]
    sizes = (MLA_Q_RANK, MLA_KV_RANK, MLA_ROPE_DIM, GQA_Q_HEADS * GQA_HEAD_DIM,
             GQA_KV_HEADS * GQA_HEAD_DIM, GQA_KV_HEADS * GQA_HEAD_DIM, 2 * D)
    parts, start = [], 0
    for s in sizes:
        parts.append(wi[:, start:start + s])
        start += s
    zq, zkv, zkr, gq, gk, gv, gate = parts
    kr_tile = jnp.pad(zkr, ((0, 0), (MLA_NOPE_DIM, LANES - MLA_QK_DIM)))
    w_in_p = jnp.concatenate([zq, kr_tile, zkv, _pad_heads(gq, GQA_Q_HEADS, GQA_HEAD_DIM),
                              _pad_heads(gk, GQA_KV_HEADS, GQA_HEAD_DIM), gate, gv], axis=1)
    ukv = w_ukv[l].reshape(MLA_KV_RANK, MLA_HEADS, MLA_NOPE_DIM + MLA_V_DIM)
    w_uk = _pad_heads(ukv[:, :, :MLA_NOPE_DIM].reshape(MLA_KV_RANK, -1), MLA_HEADS, MLA_NOPE_DIM)
    w_uv = ukv[:, :, MLA_NOPE_DIM:].reshape(MLA_KV_RANK, -1)
    pad_gain = lambda g: jnp.pad(g, (0, LANES - g.shape[0])).reshape(1, LANES)
    return {
        "w_in": w_in_p.astype(_BF),
        "q_a_norm": q_a_norm[l].reshape(1, -1),
        "w_uq": _pad_heads(w_uq[l], MLA_HEADS, MLA_QK_DIM).astype(_BF),
        "kv_a_norm": kv_a_norm[l].reshape(1, -1),
        "w_uk": w_uk.astype(_BF),
        "w_uv": w_uv.astype(_BF),
        "qk_q_norm": pad_gain(qk_q_norm[l]),
        "qk_k_norm": pad_gain(qk_k_norm[l]),
        "gqa_q_norm": pad_gain(gqa_q_norm[l]),
        "gqa_k_norm": pad_gain(gqa_k_norm[l]),
    }


def kernel(x, c, w_ada, b_ada, norm_ffn1, w_ffn1_in, w_ffn1_out, norm_mix, w_in, mla_q_a_norm, mla_w_uq,
           mla_kv_a_norm, mla_w_ukv, mla_qk_q_norm, mla_qk_k_norm, gqa_q_norm, gqa_k_norm, w_branch_mla,
           w_branch_gqa, w_out, norm_ffn2, w_ffn2_in, w_ffn2_out):
    B, S, D = x.shape
    mod = _ada(c, w_ada, b_ada).reshape(DEPTH, B, N_MOD, D)
    rope_m, rope_g = _rope_tables(S)
    w1_in, w1_out = w_ffn1_in.astype(_BF), w_ffn1_out.astype(_BF)
    w2_in, w2_out = w_ffn2_in.astype(_BF), w_ffn2_out.astype(_BF)
    wb_mla, wb_gqa, wo = w_branch_mla.astype(_BF), w_branch_gqa.astype(_BF), w_out.astype(_BF)
    norm_ffn1, norm_mix, norm_ffn2 = (g.reshape(DEPTH, 1, D) for g in (norm_ffn1, norm_mix, norm_ffn2))

    h = x
    for l in range(DEPTH):
        h = _ffn(h, mod, norm_ffn1, w1_in, w1_out, layer=l, mod0=0)
        wm = _pack_mix_weights(l, w_in, mla_q_a_norm, mla_w_uq, mla_kv_a_norm, mla_w_ukv,
                               mla_qk_q_norm, mla_qk_k_norm, gqa_q_norm, gqa_k_norm)
        qm, km, vm, qg, kg, vg, gate = _mix_in(h, mod, norm_mix, wm, rope_m, rope_g, layer=l)
        o_mla = _attn_mla(qm, km, vm)
        o_gqa = _attn_gqa(qg, kg, vg)
        h = _mix_out(h, mod, o_mla, o_gqa, gate, wb_mla, wb_gqa, wo, layer=l)
        h = _ffn(h, mod, norm_ffn2, w2_in, w2_out, layer=l, mod0=6)
    return h
```

```python
import functools
import math

import jax
import jax.numpy as jnp
import numpy as np
from jax import lax
from jax.experimental import pallas as pl
from jax.experimental.pallas import tpu as pltpu

D_MODEL = 1024
DEPTH = 2
GRID_W = 64
ROPE_THETA = 10000.0
NORM_EPS = 1e-6
MLA_HEADS = 8
MLA_Q_RANK = 384
MLA_KV_RANK = 256
MLA_NOPE_DIM = 64
MLA_ROPE_DIM = 32
MLA_V_DIM = 64
MLA_QK_DIM = MLA_NOPE_DIM + MLA_ROPE_DIM
GQA_Q_HEADS = 8
GQA_KV_HEADS = 2
GQA_GROUP = GQA_Q_HEADS // GQA_KV_HEADS
GQA_HEAD_DIM = 64
D_FF = 2816
N_MOD = 9

LANES = 128
HALF = LANES // 2
ROT = 16
BF16_ROWS = 16
LOG2E = math.log2(math.e)

HEAD_V = 64
V_ROWS = HEAD_V + BF16_ROWS
N_Q = MLA_HEADS + GQA_Q_HEADS
N_KV = MLA_HEADS + GQA_KV_HEADS

_MLA_LANE_DIM = np.full(LANES, -1, np.int64)
_MLA_LANE_DIM[0:ROT] = MLA_NOPE_DIM + np.arange(ROT)
_MLA_LANE_DIM[HALF:HALF + ROT] = MLA_NOPE_DIM + ROT + np.arange(ROT)
_MLA_LANE_DIM[ROT:HALF] = np.arange(HALF - ROT)
_MLA_LANE_DIM[HALF + ROT:HALF + 2 * ROT] = HALF - ROT + np.arange(ROT)
_GQA_LANE_DIM = np.full(LANES, -1, np.int64)
_GQA_LANE_DIM[0:ROT] = np.arange(ROT)
_GQA_LANE_DIM[ROT:2 * ROT] = 2 * ROT + np.arange(ROT)
_GQA_LANE_DIM[HALF:HALF + ROT] = ROT + np.arange(ROT)
_GQA_LANE_DIM[HALF + ROT:HALF + 2 * ROT] = 3 * ROT + np.arange(ROT)

_C_Q = 0
_C_KR = _C_Q + MLA_Q_RANK
_C_KV = _C_KR + LANES
_C_GQ = _C_KV + MLA_KV_RANK
_C_GK = _C_GQ + GQA_Q_HEADS * LANES
_C_GATE = _C_GK + GQA_KV_HEADS * LANES
_C_GV = _C_GATE + 2 * D_MODEL
_C_END = _C_GV + GQA_KV_HEADS * GQA_HEAD_DIM

VMEM_LIMIT = 56 * 1024 * 1024

FFN_TM = 1024
FFN_TSUB = 256
FFN_TF = 256
MIX_TM = 512
ATT_TQ = 256
MAX_ROWS = 32
KEY_CHUNK = 512
_DONE = object()
OUT_TM = 512

_BF = jnp.bfloat16
_F32 = jnp.float32


def _params():
    return pltpu.CompilerParams(dimension_semantics=("arbitrary", "arbitrary"),
                                vmem_limit_bytes=VMEM_LIMIT)


def _dot(a, b):
    return jnp.dot(a, b, preferred_element_type=_F32)


def _dot_nt(a, b):
    return lax.dot_general(a, b, (((1,), (1,)), ((), ())), preferred_element_type=_F32)


def _resident(block_shape, index_map):
    return pl.BlockSpec(block_shape, index_map, pipeline_mode=pl.Buffered(1))


def _rms(x, inv_n):
    return x * lax.rsqrt(jnp.sum(x * x, axis=-1, keepdims=True) * inv_n + NORM_EPS)


def _ada_kernel(c_ref, w_ref, b_ref, o_ref):
    c = c_ref[...]
    c_act = c / (1.0 + jnp.exp(-c))
    o_ref[0] = _dot(c_act.astype(_BF), w_ref[0].astype(_BF)) + b_ref[0]


def _ada(c, w_ada, b_ada):
    L, D, N = w_ada.shape
    B = c.shape[0]
    tn = D_MODEL
    return pl.pallas_call(
        _ada_kernel,
        out_shape=jax.ShapeDtypeStruct((L, B, N), _F32),
        grid=(L, N // tn),
        in_specs=[
            pl.BlockSpec((B, D), lambda l, j: (0, 0)),
            pl.BlockSpec((1, D, tn), lambda l, j: (l, 0, j)),
            pl.BlockSpec((1, 1, tn), lambda l, j: (l, 0, j)),
        ],
        out_specs=pl.BlockSpec((1, B, tn), lambda l, j: (l, 0, j)),
        compiler_params=_params(),
        name="ada",
    )(c, w_ada, b_ada.reshape(L, 1, N))


def _ffn_kernel(h_ref, mod_ref, g_ref, win_ref, wout_ref, o_ref, act_ref, *, mod0):
    shift = mod_ref[0, 0, mod0:mod0 + 1, :]
    scale1 = (1.0 + mod_ref[0, 0, mod0 + 1:mod0 + 2, :]) * g_ref[...]
    half_gate = 0.5 * mod_ref[0, 0, mod0 + 2:mod0 + 3, :]

    def sub(r, carry):
        r0 = pl.multiple_of(r * FFN_TSUB, FFN_TSUB)
        x = h_ref[0, pl.ds(r0, FFN_TSUB), :]
        u = (_rms(x, 1.0 / D_MODEL) * scale1 + shift).astype(_BF)
        for c0 in range(0, D_FF, FFN_TF):
            a = _dot(u, win_ref[:, c0:c0 + FFN_TF])
            b = _dot(u, win_ref[:, D_FF + c0:D_FF + c0 + FFN_TF])
            act_ref[:, c0:c0 + FFN_TF] = (a / (1.0 + jnp.exp(-a)) * b).astype(_BF)
        y = _dot(act_ref[...], wout_ref[...])
        o_ref[0, pl.ds(r0, FFN_TSUB), :] = x + half_gate * y
        return carry

    lax.fori_loop(0, FFN_TM // FFN_TSUB, sub, 0)


def _ffn(h, mod, gain, w_in, w_out, *, layer, mod0):
    B, S, D = h.shape
    kern = functools.partial(_ffn_kernel, mod0=mod0)
    return pl.pallas_call(
        kern,
        out_shape=jax.ShapeDtypeStruct((B, S, D), _F32),
        grid=(B, S // FFN_TM),
        in_specs=[
            pl.BlockSpec((1, FFN_TM, D), lambda b, i: (b, i, 0)),
            pl.BlockSpec((1, 1, N_MOD, D), lambda b, i: (layer, b, 0, 0)),
            pl.BlockSpec((None, 1, D), lambda b, i: (layer, 0, 0)),
            _resident((None, D, 2 * D_FF), lambda b, i: (layer, 0, 0)),
            _resident((None, D_FF, D), lambda b, i: (layer, 0, 0)),
        ],
        out_specs=pl.BlockSpec((1, FFN_TM, D), lambda b, i: (b, i, 0)),
        scratch_shapes=[pltpu.VMEM((FFN_TSUB, D_FF), _BF)],
        compiler_params=_params(),
        name="ffn",
    )(h, mod, gain, w_in, w_out)


def _norm_rope_heads(x, bd, inv_n, gain, cos, sin, out_ref, base, add=None):
    for p0 in range(0, x.shape[1], 2 * LANES):
        xs = x[:, p0:p0 + 2 * LANES]
        if add is not None:
            xs = xs + add
        ss = _dot((xs * xs).astype(_BF), bd)
        xn = xs * lax.rsqrt(ss * inv_n + NORM_EPS)
        for j in range(2):
            xh = xn[:, j * LANES:(j + 1) * LANES] * gain
            roped = xh * cos + pltpu.roll(xh, HALF, 1) * sin
            out_ref[0, base + p0 // LANES + j] = roped.astype(_BF)


def _mix_in_kernel(h_ref, mod_ref, g_ref, win_ref, qan_ref, wuq_ref, kvan_ref, wuk_ref, wuv_ref,
                   qn_ref, kn_ref, gqn_ref, gkn_ref, rm_ref, rg_ref, bd_ref,
                   q_ref, k_ref, v_ref, gate_ref):
    tm = h_ref.shape[1]
    shift = mod_ref[0, 0, 3:4, :]
    scale1 = (1.0 + mod_ref[0, 0, 4:5, :]) * g_ref[...]
    u = (_rms(h_ref[0], 1.0 / D_MODEL) * scale1 + shift).astype(_BF)

    def proj(c0, c1):
        return _dot(u, win_ref[:, c0:c1])

    bd = bd_ref[...]
    cos_m, sin_m = rm_ref[0], rm_ref[1]
    cos_g, sin_g = rg_ref[0], rg_ref[1]

    zq = proj(_C_Q, _C_KV)
    q_lat = (_rms(zq[:, :MLA_Q_RANK], 1.0 / MLA_Q_RANK) * qan_ref[...]).astype(_BF)
    k_pe = zq[:, MLA_Q_RANK:]
    _norm_rope_heads(_dot(q_lat, wuq_ref[...]), bd, 1.0 / MLA_QK_DIM, qn_ref[...], cos_m, sin_m, q_ref, 0)

    zkv = proj(_C_KV, _C_GQ)
    kv_lat = (_rms(zkv, 1.0 / MLA_KV_RANK) * kvan_ref[...]).astype(_BF)
    _norm_rope_heads(_dot(kv_lat, wuk_ref[...]), bd, 1.0 / MLA_QK_DIM, kn_ref[...], cos_m, sin_m, k_ref, 0,
                     add=jnp.concatenate([k_pe, k_pe], axis=1))
    v_t = jnp.concatenate([_dot(kv_lat, wuv_ref[...]), proj(_C_GV, _C_END)], axis=1).T
    ones_rows = jnp.where(lax.broadcasted_iota(jnp.int32, (BF16_ROWS, tm), 0) == 0, 1.0, 0.0).astype(_BF)
    for kv in range(N_KV):
        v_ref[0, kv * V_ROWS:kv * V_ROWS + HEAD_V, :] = v_t[kv * HEAD_V:(kv + 1) * HEAD_V, :].astype(_BF)
        v_ref[0, kv * V_ROWS + HEAD_V:(kv + 1) * V_ROWS, :] = ones_rows

    _norm_rope_heads(proj(_C_GQ, _C_GK), bd, 1.0 / GQA_HEAD_DIM, gqn_ref[...], cos_g, sin_g, q_ref, MLA_HEADS)
    _norm_rope_heads(proj(_C_GK, _C_GATE), bd, 1.0 / GQA_HEAD_DIM, gkn_ref[...], cos_g, sin_g, k_ref, MLA_HEADS)

    zg = proj(_C_GATE, _C_GV)
    gate_ref[0] = (1.0 / (1.0 + jnp.exp(-zg))).astype(_BF)


def _mix_in(h, mod, gain, w, rope_m, rope_g, bd, *, layer):
    B, S, D = h.shape
    tm = MIX_TM
    full = lambda shape: _resident(shape, lambda b, i: (0,) * len(shape))
    out_shape = (
        jax.ShapeDtypeStruct((B, N_Q, S, LANES), _BF),
        jax.ShapeDtypeStruct((B, N_KV, S, LANES), _BF),
        jax.ShapeDtypeStruct((B, N_KV * V_ROWS, S), _BF),
        jax.ShapeDtypeStruct((B, S, 2 * D), _BF),
    )
    out_specs = (
        pl.BlockSpec((1, N_Q, tm, LANES), lambda b, i: (b, 0, i, 0)),
        pl.BlockSpec((1, N_KV, tm, LANES), lambda b, i: (b, 0, i, 0)),
        pl.BlockSpec((1, N_KV * V_ROWS, tm), lambda b, i: (b, 0, i)),
        pl.BlockSpec((1, tm, 2 * D), lambda b, i: (b, i, 0)),
    )
    names = ("w_in", "q_a_norm", "w_uq", "kv_a_norm", "w_uk", "w_uv",
             "qk_q_norm", "qk_k_norm", "gqa_q_norm", "gqa_k_norm")
    in_specs = [
        pl.BlockSpec((1, tm, D), lambda b, i: (b, i, 0)),
        pl.BlockSpec((1, 1, N_MOD, D), lambda b, i: (layer, b, 0, 0)),
        pl.BlockSpec((None, 1, D), lambda b, i: (layer, 0, 0)),
        *[full(w[n].shape) for n in names],
        pl.BlockSpec((2, tm, LANES), lambda b, i: (0, i, 0)),
        pl.BlockSpec((2, tm, LANES), lambda b, i: (0, i, 0)),
        full(bd.shape),
    ]
    return pl.pallas_call(
        _mix_in_kernel,
        out_shape=out_shape,
        grid=(B, S // tm),
        in_specs=in_specs,
        out_specs=out_specs,
        compiler_params=_params(),
        name="mix_in",
    )(h, mod, gain, *[w[n] for n in names], rope_m, rope_g, bd)


def _kv_index(u):
    if isinstance(u, int):
        return u if u < MLA_HEADS else MLA_HEADS + (u - MLA_HEADS) // GQA_GROUP
    return jnp.where(u < MLA_HEADS, u, MLA_HEADS + lax.shift_right_logical(u - MLA_HEADS, 2))


def _attn_kernel(q_ref, k_ref, v_ref, o_ref, s0_ref, s1_ref, m_ref, ot_ref):
    s_refs = (s0_ref, s1_ref)

    n_keys = k_ref.shape[2]
    chunks = [(c0, KEY_CHUNK) for c0 in range(0, n_keys, KEY_CHUNK)]

    def scores(u, slot):
        kv = _kv_index(u)
        q = q_ref[0, u]
        acc = None
        for c0, cn in chunks:
            s_c = _dot_nt(k_ref[0, kv, c0:c0 + cn, :], q)
            s_refs[slot][c0:c0 + cn, :] = s_c
            for r0 in range(0, cn, MAX_ROWS):
                blk = s_c[r0:r0 + MAX_ROWS]
                acc = blk if acc is None else jnp.maximum(acc, blk)
            yield
        m_ref[slot] = jnp.max(acc, axis=0, keepdims=True)
        yield

    def combine(u, slot):
        r0 = _kv_index(u) * V_ROWS
        c_out = u * HEAD_V
        if not isinstance(u, int):
            r0 = pl.multiple_of(r0, BF16_ROWS)
            c_out = pl.multiple_of(c_out, HEAD_V)
        m = m_ref[slot]
        o_aug = None
        for c0, cn in chunks:
            p = jnp.exp2(s_refs[slot][c0:c0 + cn, :] - m).astype(_BF)
            part = _dot(v_ref[0, pl.ds(r0, V_ROWS), c0:c0 + cn], p)
            o_aug = part if o_aug is None else o_aug + part
            yield
        ot_ref[pl.ds(c_out, HEAD_V), :] = o_aug[:HEAD_V] / o_aug[HEAD_V:HEAD_V + 1]
        yield

    def run(*stages):
        live = list(stages)
        while live:
            for g in list(live):
                if next(g, _DONE) is _DONE:
                    live.remove(g)

    run(scores(0, 0))

    def pair(i, carry):
        u = 2 * i
        run(scores(u + 1, 1), combine(u, 0))
        run(scores(u + 2, 0), combine(u + 1, 1))
        return carry

    lax.fori_loop(0, N_Q // 2 - 1, pair, 0)
    run(scores(N_Q - 1, 1), combine(N_Q - 2, 0))
    run(combine(N_Q - 1, 1))
    o_ref[0] = ot_ref[...].T.astype(_BF)


def _attn(q, k, v_t):
    B, _, S, _ = q.shape
    tq = ATT_TQ
    return pl.pallas_call(
        _attn_kernel,
        out_shape=jax.ShapeDtypeStruct((B, S, N_Q * HEAD_V), _BF),
        grid=(B, S // tq),
        in_specs=[
            pl.BlockSpec((1, N_Q, tq, LANES), lambda b, i: (b, 0, i, 0)),
            pl.BlockSpec((1, N_KV, S, LANES), lambda b, i: (b, 0, 0, 0)),
            pl.BlockSpec((1, N_KV * V_ROWS, S), lambda b, i: (b, 0, 0)),
        ],
        out_specs=pl.BlockSpec((1, tq, N_Q * HEAD_V), lambda b, i: (b, i, 0)),
        scratch_shapes=[
            pltpu.VMEM((S, tq), _F32),
            pltpu.VMEM((S, tq), _F32),
            pltpu.VMEM((2, 1, tq), _F32),
            pltpu.VMEM((N_Q * HEAD_V, tq), _F32),
        ],
        compiler_params=_params(),
        name="attn",
    )(q, k, v_t)


def _mix_out_kernel(h_ref, mod_ref, o_ref_in, gate_ref, wbm_ref, wbg_ref, wo_ref, o_ref):
    g2 = mod_ref[0, 0, 5:6, :]
    n_mla = MLA_HEADS * HEAD_V
    bm = _dot(o_ref_in[0, :, :n_mla], wbm_ref[...])
    bg = _dot(o_ref_in[0, :, n_mla:], wbg_ref[...])
    merged = gate_ref[0, :, :D_MODEL] * bm + gate_ref[0, :, D_MODEL:] * bg
    o_ref[0] = h_ref[0] + g2 * _dot(merged.astype(_BF), wo_ref[...])


def _mix_out(h, mod, o_att, gate, wb_mla, wb_gqa, w_out, *, layer):
    B, S, D = h.shape
    tm = OUT_TM
    row = lambda n: pl.BlockSpec((1, tm, n), lambda b, i: (b, i, 0))
    wspec = lambda a: _resident((None,) + a.shape[1:], lambda b, i: (layer, 0, 0))
    return pl.pallas_call(
        _mix_out_kernel,
        out_shape=jax.ShapeDtypeStruct((B, S, D), _F32),
        grid=(B, S // tm),
        in_specs=[
            row(D),
            pl.BlockSpec((1, 1, N_MOD, D), lambda b, i: (layer, b, 0, 0)),
            row(o_att.shape[-1]), row(2 * D),
            wspec(wb_mla), wspec(wb_gqa), wspec(w_out),
        ],
        out_specs=row(D),
        compiler_params=_params(),
        name="mix_out",
    )(h, mod, o_att, gate, wb_mla, wb_gqa, w_out)


def _rope_angles(pos, dim):
    inv = ROPE_THETA ** (-jnp.arange(0, dim, 2, dtype=_F32) / dim)
    ang = pos.astype(_F32)[:, None] * inv[None, :]
    return jnp.cos(ang), jnp.sin(ang)


def _rope_tables(S):
    t = jnp.arange(S)
    one = lambda n: jnp.ones((S, n), _F32)
    zero = lambda n: jnp.zeros((S, n), _F32)
    c, s = _rope_angles(t, MLA_ROPE_DIM)
    rope_m = jnp.stack([
        jnp.concatenate([c, one(HALF - ROT), c, one(HALF - ROT)], axis=1),
        jnp.concatenate([-s, zero(HALF - ROT), s, zero(HALF - ROT)], axis=1),
    ])
    cr, sr = _rope_angles(t // GRID_W, GQA_HEAD_DIM // 2)
    cc, sc = _rope_angles(t % GRID_W, GQA_HEAD_DIM // 2)
    rope_g = jnp.stack([
        jnp.concatenate([cr, cc, one(HALF - 2 * ROT), cr, cc, one(HALF - 2 * ROT)], axis=1),
        jnp.concatenate([-sr, -sc, zero(HALF - 2 * ROT), sr, sc, zero(HALF - 2 * ROT)], axis=1),
    ])
    return rope_m, rope_g


def _to_lanes(w, heads, dim, lane_dim):
    lead = w.shape[:-1]
    w = w.reshape(lead + (heads, dim))
    w = jnp.concatenate([w, jnp.zeros(lead + (heads, 1), w.dtype)], axis=-1)
    w = jnp.take(w, jnp.asarray(np.where(lane_dim < 0, dim, lane_dim)), axis=-1)
    return w.reshape(lead + (heads * LANES,))


def _pack_mix_weights(l, w_in, q_a_norm, w_uq, kv_a_norm, w_ukv, qk_q_norm, qk_k_norm, gqa_q_norm, gqa_k_norm):
    D = D_MODEL
    wi = w_in[l]
    sizes = (MLA_Q_RANK, MLA_KV_RANK, MLA_ROPE_DIM, GQA_Q_HEADS * GQA_HEAD_DIM,
             GQA_KV_HEADS * GQA_HEAD_DIM, GQA_KV_HEADS * GQA_HEAD_DIM, 2 * D)
    parts, start = [], 0
    for s in sizes:
        parts.append(wi[:, start:start + s])
        start += s
    zq, zkv, zkr, gq, gk, gv, gate = parts
    kr_tile = _to_lanes(jnp.pad(zkr, ((0, 0), (MLA_NOPE_DIM, 0))), 1, MLA_QK_DIM, _MLA_LANE_DIM)
    w_in_p = jnp.concatenate([zq, kr_tile, zkv, _to_lanes(gq, GQA_Q_HEADS, GQA_HEAD_DIM, _GQA_LANE_DIM),
                              _to_lanes(gk, GQA_KV_HEADS, GQA_HEAD_DIM, _GQA_LANE_DIM), gate, gv], axis=1)
    ukv = w_ukv[l].reshape(MLA_KV_RANK, MLA_HEADS, MLA_NOPE_DIM + MLA_V_DIM)
    uk = jnp.pad(ukv[:, :, :MLA_NOPE_DIM], ((0, 0), (0, 0), (0, MLA_ROPE_DIM))).reshape(MLA_KV_RANK, -1)
    w_uv = ukv[:, :, MLA_NOPE_DIM:].reshape(MLA_KV_RANK, -1)
    return {
        "w_in": w_in_p.astype(_BF),
        "q_a_norm": q_a_norm[l].reshape(1, -1),
        "w_uq": _to_lanes(w_uq[l], MLA_HEADS, MLA_QK_DIM, _MLA_LANE_DIM).astype(_BF),
        "kv_a_norm": kv_a_norm[l].reshape(1, -1),
        "w_uk": _to_lanes(uk, MLA_HEADS, MLA_QK_DIM, _MLA_LANE_DIM).astype(_BF),
        "w_uv": w_uv.astype(_BF),
        "qk_q_norm": _to_lanes(qk_q_norm[l] * (MLA_QK_DIM ** -0.5 * LOG2E), 1, MLA_QK_DIM, _MLA_LANE_DIM)[None],
        "qk_k_norm": _to_lanes(qk_k_norm[l], 1, MLA_QK_DIM, _MLA_LANE_DIM)[None],
        "gqa_q_norm": _to_lanes(gqa_q_norm[l] * (GQA_HEAD_DIM ** -0.5 * LOG2E), 1, GQA_HEAD_DIM, _GQA_LANE_DIM)[None],
        "gqa_k_norm": _to_lanes(gqa_k_norm[l], 1, GQA_HEAD_DIM, _GQA_LANE_DIM)[None],
    }


def kernel(x, c, w_ada, b_ada, norm_ffn1, w_ffn1_in, w_ffn1_out, norm_mix, w_in, mla_q_a_norm, mla_w_uq,
           mla_kv_a_norm, mla_w_ukv, mla_qk_q_norm, mla_qk_k_norm, gqa_q_norm, gqa_k_norm, w_branch_mla,
           w_branch_gqa, w_out, norm_ffn2, w_ffn2_in, w_ffn2_out):
    B, S, D = x.shape
    mod = _ada(c, w_ada, b_ada).reshape(DEPTH, B, N_MOD, D)
    rope_m, rope_g = _rope_tables(S)
    ones_blk = jnp.ones((LANES, LANES), _BF)
    zero_blk = jnp.zeros((LANES, LANES), _BF)
    bd = jnp.block([[ones_blk, zero_blk], [zero_blk, ones_blk]])
    w1_in, w1_out = w_ffn1_in.astype(_BF), w_ffn1_out.astype(_BF)
    w2_in, w2_out = w_ffn2_in.astype(_BF), w_ffn2_out.astype(_BF)
    wb_mla, wb_gqa, wo = w_branch_mla.astype(_BF), w_branch_gqa.astype(_BF), w_out.astype(_BF)
    norm_ffn1, norm_mix, norm_ffn2 = (g.reshape(DEPTH, 1, D) for g in (norm_ffn1, norm_mix, norm_ffn2))

    h = x
    for l in range(DEPTH):
        h = _ffn(h, mod, norm_ffn1, w1_in, w1_out, layer=l, mod0=0)
        wm = _pack_mix_weights(l, w_in, mla_q_a_norm, mla_w_uq, mla_kv_a_norm, mla_w_ukv,
                               mla_qk_q_norm, mla_qk_k_norm, gqa_q_norm, gqa_k_norm)
        q, k, v_t, gate = _mix_in(h, mod, norm_mix, wm, rope_m, rope_g, bd, layer=l)
        o_att = _attn(q, k, v_t)
        h = _mix_out(h, mod, o_att, gate, wb_mla, wb_gqa, wo, layer=l)
        h = _ffn(h, mod, norm_ffn2, w2_in, w2_out, layer=l, mod0=6)
    return h
```

```python
import functools
import math

import jax
import jax.numpy as jnp
import numpy as np
from jax import lax
from jax.experimental import pallas as pl
from jax.experimental.pallas import tpu as pltpu

D_MODEL = 1024
DEPTH = 2
GRID_W = 64
ROPE_THETA = 10000.0
NORM_EPS = 1e-6
MLA_HEADS = 8
MLA_Q_RANK = 384
MLA_KV_RANK = 256
MLA_NOPE_DIM = 64
MLA_ROPE_DIM = 32
MLA_V_DIM = 64
MLA_QK_DIM = MLA_NOPE_DIM + MLA_ROPE_DIM
GQA_Q_HEADS = 8
GQA_KV_HEADS = 2
GQA_GROUP = GQA_Q_HEADS // GQA_KV_HEADS
GQA_HEAD_DIM = 64
D_FF = 2816
N_MOD = 9

LANES = 128
HALF = LANES // 2
ROT = 16
BF16_ROWS = 16
LOG2E = math.log2(math.e)

HEAD_V = 64
V_ROWS = HEAD_V + BF16_ROWS
N_Q = MLA_HEADS + GQA_Q_HEADS
N_KV = MLA_HEADS + GQA_KV_HEADS
N_QT = MLA_HEADS + GQA_Q_HEADS // 2
N_KT = MLA_HEADS + 2 * GQA_KV_HEADS

_MLA_LANE_DIM = np.full(LANES, -1, np.int64)
_MLA_LANE_DIM[0:ROT] = MLA_NOPE_DIM + np.arange(ROT)
_MLA_LANE_DIM[HALF:HALF + ROT] = MLA_NOPE_DIM + ROT + np.arange(ROT)
_MLA_LANE_DIM[ROT:HALF] = np.arange(HALF - ROT)
_MLA_LANE_DIM[HALF + ROT:HALF + 2 * ROT] = HALF - ROT + np.arange(ROT)
_GQA_LANE_PARITY = (np.arange(LANES) // (2 * ROT)) % 2
_GQA_LANE_DIM = (2 * ROT * ((np.arange(LANES) // ROT) % 2)
                 + ROT * (np.arange(LANES) // HALF)
                 + np.arange(LANES) % ROT)

_C_Q = 0
_C_KR = _C_Q + MLA_Q_RANK
_C_KV = _C_KR + LANES
_C_GQ = _C_KV + MLA_KV_RANK
_C_GK = _C_GQ + (GQA_Q_HEADS // 2) * LANES
_C_GATE = _C_GK + GQA_KV_HEADS * LANES
_C_GV = _C_GATE + 2 * D_MODEL
_C_END = _C_GV + GQA_KV_HEADS * GQA_HEAD_DIM

VMEM_LIMIT = 56 * 1024 * 1024

FFN_TM = 1024
FFN_TSUB = 256
FFN_TF = 256
MIX_TM = 512
ATT_TQ = 256
MAX_ROWS = 32
KEY_CHUNK = 512
_DONE = object()
OUT_TM = 512

_BF = jnp.bfloat16
_F32 = jnp.float32


def _params():
    return pltpu.CompilerParams(dimension_semantics=("arbitrary", "arbitrary"),
                                vmem_limit_bytes=VMEM_LIMIT)


def _dot(a, b):
    return jnp.dot(a, b, preferred_element_type=_F32)


def _dot_nt(a, b):
    return lax.dot_general(a, b, (((1,), (1,)), ((), ())), preferred_element_type=_F32)


def _resident(block_shape, index_map):
    return pl.BlockSpec(block_shape, index_map, pipeline_mode=pl.Buffered(1))


def _rms(x, inv_n):
    return x * lax.rsqrt(jnp.sum(x * x, axis=-1, keepdims=True) * inv_n + NORM_EPS)


def _ada_kernel(c_ref, w_ref, b_ref, o_ref):
    c = c_ref[...]
    c_act = c / (1.0 + jnp.exp(-c))
    o_ref[0] = _dot(c_act.astype(_BF), w_ref[0].astype(_BF)) + b_ref[0]


def _ada(c, w_ada, b_ada):
    L, D, N = w_ada.shape
    B = c.shape[0]
    tn = D_MODEL
    return pl.pallas_call(
        _ada_kernel,
        out_shape=jax.ShapeDtypeStruct((L, B, N), _F32),
        grid=(L, N // tn),
        in_specs=[
            pl.BlockSpec((B, D), lambda l, j: (0, 0)),
            pl.BlockSpec((1, D, tn), lambda l, j: (l, 0, j)),
            pl.BlockSpec((1, 1, tn), lambda l, j: (l, 0, j)),
        ],
        out_specs=pl.BlockSpec((1, B, tn), lambda l, j: (l, 0, j)),
        compiler_params=_params(),
        name="ada",
    )(c, w_ada, b_ada.reshape(L, 1, N))


def _ffn_kernel(h_ref, mod_ref, g_ref, win_ref, wout_ref, o_ref, act_ref, *, mod0):
    shift = mod_ref[0, 0, mod0:mod0 + 1, :]
    scale1 = (1.0 + mod_ref[0, 0, mod0 + 1:mod0 + 2, :]) * g_ref[...]
    half_gate = 0.5 * mod_ref[0, 0, mod0 + 2:mod0 + 3, :]

    def sub(r, carry):
        r0 = pl.multiple_of(r * FFN_TSUB, FFN_TSUB)
        x = h_ref[0, pl.ds(r0, FFN_TSUB), :]
        u = (_rms(x, 1.0 / D_MODEL) * scale1 + shift).astype(_BF)
        for c0 in range(0, D_FF, FFN_TF):
            a = _dot(u, win_ref[:, c0:c0 + FFN_TF])
            b = _dot(u, win_ref[:, D_FF + c0:D_FF + c0 + FFN_TF])
            act_ref[:, c0:c0 + FFN_TF] = (a / (1.0 + jnp.exp(-a)) * b).astype(_BF)
        y = _dot(act_ref[...], wout_ref[...])
        o_ref[0, pl.ds(r0, FFN_TSUB), :] = x + half_gate * y
        return carry

    lax.fori_loop(0, FFN_TM // FFN_TSUB, sub, 0)


def _ffn(h, mod, gain, w_in, w_out, *, layer, mod0):
    B, S, D = h.shape
    kern = functools.partial(_ffn_kernel, mod0=mod0)
    return pl.pallas_call(
        kern,
        out_shape=jax.ShapeDtypeStruct((B, S, D), _F32),
        grid=(B, S // FFN_TM),
        in_specs=[
            pl.BlockSpec((1, FFN_TM, D), lambda b, i: (b, i, 0)),
            pl.BlockSpec((1, 1, N_MOD, D), lambda b, i: (layer, b, 0, 0)),
            pl.BlockSpec((None, 1, D), lambda b, i: (layer, 0, 0)),
            _resident((None, D, 2 * D_FF), lambda b, i: (layer, 0, 0)),
            _resident((None, D_FF, D), lambda b, i: (layer, 0, 0)),
        ],
        out_specs=pl.BlockSpec((1, FFN_TM, D), lambda b, i: (b, i, 0)),
        scratch_shapes=[pltpu.VMEM((FFN_TSUB, D_FF), _BF)],
        compiler_params=_params(),
        name="ffn",
    )(h, mod, gain, w_in, w_out)


def _norm_rope_heads(x, bd, inv_n, gain, cos, sin, add=None):
    tiles = []
    for p0 in range(0, x.shape[1], 2 * LANES):
        xs = x[:, p0:p0 + 2 * LANES]
        if add is not None:
            xs = xs + add
        ss = _dot((xs * xs).astype(_BF), bd)
        xn = xs * lax.rsqrt(ss * inv_n + NORM_EPS)
        for j in range(2):
            xh = xn[:, j * LANES:(j + 1) * LANES] * gain
            tiles.append(xh * cos + pltpu.roll(xh, HALF, 1) * sin)
    return tiles


def _mix_in_kernel(h_ref, mod_ref, g_ref, win_ref, qan_ref, wuq_ref, kvan_ref, wuk_ref, wuv_ref,
                   qn_ref, kn_ref, gqn_ref, gkn_ref, rm_ref, rg_ref, bdm_ref, bdg_ref,
                   q_ref, k_ref, v_ref, gate_ref):
    tm = h_ref.shape[1]
    shift = mod_ref[0, 0, 3:4, :]
    scale1 = (1.0 + mod_ref[0, 0, 4:5, :]) * g_ref[...]
    u = (_rms(h_ref[0], 1.0 / D_MODEL) * scale1 + shift).astype(_BF)

    def proj(c0, c1):
        return _dot(u, win_ref[:, c0:c1])

    bd_m, bd_g = bdm_ref[...], bdg_ref[...]
    cos_m, sin_m = rm_ref[0], rm_ref[1]
    cos_g, sin_g = rg_ref[0], rg_ref[1]

    zq = proj(_C_Q, _C_KV)
    q_lat = (_rms(zq[:, :MLA_Q_RANK], 1.0 / MLA_Q_RANK) * qan_ref[...]).astype(_BF)
    k_pe = zq[:, MLA_Q_RANK:]
    for t, tile in enumerate(_norm_rope_heads(_dot(q_lat, wuq_ref[...]), bd_m, 1.0 / MLA_QK_DIM, qn_ref[...],
                                              cos_m, sin_m)):
        q_ref[0, t] = tile.astype(_BF)

    zkv = proj(_C_KV, _C_GQ)
    kv_lat = (_rms(zkv, 1.0 / MLA_KV_RANK) * kvan_ref[...]).astype(_BF)
    for t, tile in enumerate(_norm_rope_heads(_dot(kv_lat, wuk_ref[...]), bd_m, 1.0 / MLA_QK_DIM, kn_ref[...],
                                              cos_m, sin_m, add=jnp.concatenate([k_pe, k_pe], axis=1))):
        k_ref[0, t] = tile.astype(_BF)
    v_t = jnp.concatenate([_dot(kv_lat, wuv_ref[...]), proj(_C_GV, _C_END)], axis=1).T
    ones_rows = jnp.where(lax.broadcasted_iota(jnp.int32, (BF16_ROWS, tm), 0) == 0, 1.0, 0.0).astype(_BF)
    for kv in range(N_KV):
        v_ref[0, kv * V_ROWS:kv * V_ROWS + HEAD_V, :] = v_t[kv * HEAD_V:(kv + 1) * HEAD_V, :].astype(_BF)
        v_ref[0, kv * V_ROWS + HEAD_V:(kv + 1) * V_ROWS, :] = ones_rows

    for t, tile in enumerate(_norm_rope_heads(proj(_C_GQ, _C_GK), bd_g, 1.0 / GQA_HEAD_DIM, gqn_ref[...],
                                              cos_g, sin_g)):
        q_ref[0, MLA_HEADS + t] = tile.astype(_BF)
    lane_parity = lax.bitwise_and(lax.shift_right_logical(lax.broadcasted_iota(jnp.int32, (1, LANES), 1), 5), 1)
    for kv, tile in enumerate(_norm_rope_heads(proj(_C_GK, _C_GATE), bd_g, 1.0 / GQA_HEAD_DIM, gkn_ref[...],
                                               cos_g, sin_g)):
        for par in range(2):
            k_ref[0, MLA_HEADS + 2 * kv + par] = jnp.where(lane_parity == par, tile, 0.0).astype(_BF)

    zg = proj(_C_GATE, _C_GV)
    gate_ref[0] = (1.0 / (1.0 + jnp.exp(-zg))).astype(_BF)


def _mix_in(h, mod, gain, w, rope_m, rope_g, bd_m, bd_g, *, layer):
    B, S, D = h.shape
    tm = MIX_TM
    full = lambda shape: _resident(shape, lambda b, i: (0,) * len(shape))
    out_shape = (
        jax.ShapeDtypeStruct((B, N_QT, S, LANES), _BF),
        jax.ShapeDtypeStruct((B, N_KT, S, LANES), _BF),
        jax.ShapeDtypeStruct((B, N_KV * V_ROWS, S), _BF),
        jax.ShapeDtypeStruct((B, S, 2 * D), _BF),
    )
    out_specs = (
        pl.BlockSpec((1, N_QT, tm, LANES), lambda b, i: (b, 0, i, 0)),
        pl.BlockSpec((1, N_KT, tm, LANES), lambda b, i: (b, 0, i, 0)),
        pl.BlockSpec((1, N_KV * V_ROWS, tm), lambda b, i: (b, 0, i)),
        pl.BlockSpec((1, tm, 2 * D), lambda b, i: (b, i, 0)),
    )
    names = ("w_in", "q_a_norm", "w_uq", "kv_a_norm", "w_uk", "w_uv",
             "qk_q_norm", "qk_k_norm", "gqa_q_norm", "gqa_k_norm")
    in_specs = [
        pl.BlockSpec((1, tm, D), lambda b, i: (b, i, 0)),
        pl.BlockSpec((1, 1, N_MOD, D), lambda b, i: (layer, b, 0, 0)),
        pl.BlockSpec((None, 1, D), lambda b, i: (layer, 0, 0)),
        *[full(w[n].shape) for n in names],
        pl.BlockSpec((2, tm, LANES), lambda b, i: (0, i, 0)),
        pl.BlockSpec((2, tm, LANES), lambda b, i: (0, i, 0)),
        full(bd_m.shape),
        full(bd_g.shape),
    ]
    return pl.pallas_call(
        _mix_in_kernel,
        out_shape=out_shape,
        grid=(B, S // tm),
        in_specs=in_specs,
        out_specs=out_specs,
        compiler_params=_params(),
        name="mix_in",
    )(h, mod, gain, *[w[n] for n in names], rope_m, rope_g, bd_m, bd_g)


def _head_tiles(head):
    if isinstance(head, int):
        if head < MLA_HEADS:
            return head, head, head
        g = head - MLA_HEADS
        return MLA_HEADS + g // 2, MLA_HEADS + 2 * (g // GQA_GROUP) + g % 2, MLA_HEADS + g // GQA_GROUP
    g = head - MLA_HEADS
    kv = lax.shift_right_logical(g, 2)
    is_mla = head < MLA_HEADS
    return (jnp.where(is_mla, head, MLA_HEADS + lax.shift_right_logical(g, 1)),
            jnp.where(is_mla, head, MLA_HEADS + 2 * kv + lax.bitwise_and(g, 1)),
            jnp.where(is_mla, head, MLA_HEADS + kv))


def _attn_kernel(q_ref, k_ref, v_ref, o_ref, s0_ref, s1_ref, m_ref, ot_ref):
    s_refs = (s0_ref, s1_ref)
    n_keys = k_ref.shape[2]
    tq = s0_ref.shape[1]
    n_units = (q_ref.shape[2] // tq) * N_Q
    chunks = [(c0, KEY_CHUNK) for c0 in range(0, n_keys, KEY_CHUNK)]

    def unit(n):
        if isinstance(n, int):
            return n % N_Q, (n // N_Q) * tq
        head = lax.bitwise_and(n, N_Q - 1)
        return head, pl.multiple_of(lax.shift_right_logical(n, N_Q.bit_length() - 1) * tq, tq)

    def scores(n, slot):
        head, q0 = unit(n)
        q_tile, k_tile, _ = _head_tiles(head)
        q = q_ref[0, q_tile, pl.ds(q0, tq), :]
        acc = None
        for c0, cn in chunks:
            s_c = _dot_nt(k_ref[0, k_tile, c0:c0 + cn, :], q)
            s_refs[slot][c0:c0 + cn, :] = s_c
            for r0 in range(0, cn, MAX_ROWS):
                blk = s_c[r0:r0 + MAX_ROWS]
                acc = blk if acc is None else jnp.maximum(acc, blk)
            yield
        m_ref[slot] = jnp.max(acc, axis=0, keepdims=True)
        yield

    def combine(n, slot):
        head, _ = unit(n)
        r0 = _head_tiles(head)[2] * V_ROWS
        c_out = head * HEAD_V
        if not isinstance(n, int):
            r0 = pl.multiple_of(r0, BF16_ROWS)
            c_out = pl.multiple_of(c_out, HEAD_V)
        m = m_ref[slot]
        o_aug = None
        for c0, cn in chunks:
            p = jnp.exp2(s_refs[slot][c0:c0 + cn, :] - m).astype(_BF)
            part = _dot(v_ref[0, pl.ds(r0, V_ROWS), c0:c0 + cn], p)
            o_aug = part if o_aug is None else o_aug + part
            yield
        ot_ref[pl.ds(c_out, HEAD_V), :] = o_aug[:HEAD_V] / o_aug[HEAD_V:HEAD_V + 1]
        yield

    def run(*stages):
        live = list(stages)
        while live:
            for g in list(live):
                if next(g, _DONE) is _DONE:
                    live.remove(g)

    def flush(q0):
        o_ref[0, pl.ds(q0, tq), :] = ot_ref[...].T.astype(_BF)

    run(scores(0, 0))

    def pair(i, carry):
        n = 2 * i
        run(scores(n + 1, 1), combine(n, 0))
        run(scores(n + 2, 0), combine(n + 1, 1))
        head, q0 = unit(n + 1)

        @pl.when(head == N_Q - 1)
        def _():
            flush(q0)

        return carry

    lax.fori_loop(0, n_units // 2 - 1, pair, 0)
    run(scores(n_units - 1, 1), combine(n_units - 2, 0))
    run(combine(n_units - 1, 1))
    flush(unit(n_units - 1)[1])


def _attn(q, k, v_t):
    B, _, S, _ = q.shape
    tq = ATT_TQ
    return pl.pallas_call(
        _attn_kernel,
        out_shape=jax.ShapeDtypeStruct((B, S, N_Q * HEAD_V), _BF),
        grid=(B,),
        in_specs=[
            pl.BlockSpec((1, N_QT, S, LANES), lambda b: (b, 0, 0, 0)),
            pl.BlockSpec((1, N_KT, S, LANES), lambda b: (b, 0, 0, 0)),
            pl.BlockSpec((1, N_KV * V_ROWS, S), lambda b: (b, 0, 0)),
        ],
        out_specs=pl.BlockSpec((1, S, N_Q * HEAD_V), lambda b: (b, 0, 0)),
        scratch_shapes=[
            pltpu.VMEM((S, tq), _F32),
            pltpu.VMEM((S, tq), _F32),
            pltpu.VMEM((2, 1, tq), _F32),
            pltpu.VMEM((N_Q * HEAD_V, tq), _F32),
        ],
        compiler_params=pltpu.CompilerParams(dimension_semantics=("arbitrary",),
                                             vmem_limit_bytes=VMEM_LIMIT),
        name="attn",
    )(q, k, v_t)


def _mix_out_kernel(h_ref, mod_ref, o_ref_in, gate_ref, wbm_ref, wbg_ref, wo_ref, o_ref):
    g2 = mod_ref[0, 0, 5:6, :]
    n_mla = MLA_HEADS * HEAD_V
    bm = _dot(o_ref_in[0, :, :n_mla], wbm_ref[...])
    bg = _dot(o_ref_in[0, :, n_mla:], wbg_ref[...])
    merged = gate_ref[0, :, :D_MODEL] * bm + gate_ref[0, :, D_MODEL:] * bg
    o_ref[0] = h_ref[0] + g2 * _dot(merged.astype(_BF), wo_ref[...])


def _mix_out(h, mod, o_att, gate, wb_mla, wb_gqa, w_out, *, layer):
    B, S, D = h.shape
    tm = OUT_TM
    row = lambda n: pl.BlockSpec((1, tm, n), lambda b, i: (b, i, 0))
    wspec = lambda a: _resident((None,) + a.shape[1:], lambda b, i: (layer, 0, 0))
    return pl.pallas_call(
        _mix_out_kernel,
        out_shape=jax.ShapeDtypeStruct((B, S, D), _F32),
        grid=(B, S // tm),
        in_specs=[
            row(D),
            pl.BlockSpec((1, 1, N_MOD, D), lambda b, i: (layer, b, 0, 0)),
            row(o_att.shape[-1]), row(2 * D),
            wspec(wb_mla), wspec(wb_gqa), wspec(w_out),
        ],
        out_specs=row(D),
        compiler_params=_params(),
        name="mix_out",
    )(h, mod, o_att, gate, wb_mla, wb_gqa, w_out)


def _rope_angles(pos, dim):
    inv = ROPE_THETA ** (-jnp.arange(0, dim, 2, dtype=_F32) / dim)
    ang = pos.astype(_F32)[:, None] * inv[None, :]
    return jnp.cos(ang), jnp.sin(ang)


def _rope_tables(S):
    t = jnp.arange(S)
    one = lambda n: jnp.ones((S, n), _F32)
    zero = lambda n: jnp.zeros((S, n), _F32)
    c, s = _rope_angles(t, MLA_ROPE_DIM)
    rope_m = jnp.stack([
        jnp.concatenate([c, one(HALF - ROT), c, one(HALF - ROT)], axis=1),
        jnp.concatenate([-s, zero(HALF - ROT), s, zero(HALF - ROT)], axis=1),
    ])
    cr, sr = _rope_angles(t // GRID_W, GQA_HEAD_DIM // 2)
    cc, sc = _rope_angles(t % GRID_W, GQA_HEAD_DIM // 2)
    sign = jnp.where(jnp.arange(LANES) < HALF, -1.0, 1.0).astype(_F32)
    rope_g = jnp.stack([
        jnp.tile(jnp.concatenate([cr, cc], axis=1), (1, LANES // (2 * ROT))),
        jnp.tile(jnp.concatenate([sr, sc], axis=1), (1, LANES // (2 * ROT))) * sign[None, :],
    ])
    return rope_m, rope_g


def _to_lanes(w, heads, dim, lane_dim):
    lead = w.shape[:-1]
    w = w.reshape(lead + (heads, dim))
    w = jnp.concatenate([w, jnp.zeros(lead + (heads, 1), w.dtype)], axis=-1)
    w = jnp.take(w, jnp.asarray(np.where(lane_dim < 0, dim, lane_dim)), axis=-1)
    return w.reshape(lead + (heads * LANES,))


def _take_cols(w, cols):
    return jnp.take(w, jnp.asarray(cols), axis=-1)


_GQ_COLS = np.concatenate([(2 * t + _GQA_LANE_PARITY) * GQA_HEAD_DIM + _GQA_LANE_DIM
                           for t in range(GQA_Q_HEADS // 2)])
_GK_COLS = np.concatenate([kv * GQA_HEAD_DIM + _GQA_LANE_DIM for kv in range(GQA_KV_HEADS)])


def _pack_mix_weights(l, w_in, q_a_norm, w_uq, kv_a_norm, w_ukv, qk_q_norm, qk_k_norm, gqa_q_norm, gqa_k_norm):
    D = D_MODEL
    wi = w_in[l]
    sizes = (MLA_Q_RANK, MLA_KV_RANK, MLA_ROPE_DIM, GQA_Q_HEADS * GQA_HEAD_DIM,
             GQA_KV_HEADS * GQA_HEAD_DIM, GQA_KV_HEADS * GQA_HEAD_DIM, 2 * D)
    parts, start = [], 0
    for s in sizes:
        parts.append(wi[:, start:start + s])
        start += s
    zq, zkv, zkr, gq, gk, gv, gate = parts
    kr_tile = _to_lanes(jnp.pad(zkr, ((0, 0), (MLA_NOPE_DIM, 0))), 1, MLA_QK_DIM, _MLA_LANE_DIM)
    w_in_p = jnp.concatenate([zq, kr_tile, zkv, _take_cols(gq, _GQ_COLS), _take_cols(gk, _GK_COLS), gate, gv],
                             axis=1)
    ukv = w_ukv[l].reshape(MLA_KV_RANK, MLA_HEADS, MLA_NOPE_DIM + MLA_V_DIM)
    uk = jnp.pad(ukv[:, :, :MLA_NOPE_DIM], ((0, 0), (0, 0), (0, MLA_ROPE_DIM))).reshape(MLA_KV_RANK, -1)
    w_uv = ukv[:, :, MLA_NOPE_DIM:].reshape(MLA_KV_RANK, -1)
    return {
        "w_in": w_in_p.astype(_BF),
        "q_a_norm": q_a_norm[l].reshape(1, -1),
        "w_uq": _to_lanes(w_uq[l], MLA_HEADS, MLA_QK_DIM, _MLA_LANE_DIM).astype(_BF),
        "kv_a_norm": kv_a_norm[l].reshape(1, -1),
        "w_uk": _to_lanes(uk, MLA_HEADS, MLA_QK_DIM, _MLA_LANE_DIM).astype(_BF),
        "w_uv": w_uv.astype(_BF),
        "qk_q_norm": _to_lanes(qk_q_norm[l] * (MLA_QK_DIM ** -0.5 * LOG2E), 1, MLA_QK_DIM, _MLA_LANE_DIM)[None],
        "qk_k_norm": _to_lanes(qk_k_norm[l], 1, MLA_QK_DIM, _MLA_LANE_DIM)[None],
        "gqa_q_norm": _take_cols(gqa_q_norm[l] * (GQA_HEAD_DIM ** -0.5 * LOG2E), _GQA_LANE_DIM)[None],
        "gqa_k_norm": _take_cols(gqa_k_norm[l], _GQA_LANE_DIM)[None],
    }


def kernel(x, c, w_ada, b_ada, norm_ffn1, w_ffn1_in, w_ffn1_out, norm_mix, w_in, mla_q_a_norm, mla_w_uq,
           mla_kv_a_norm, mla_w_ukv, mla_qk_q_norm, mla_qk_k_norm, gqa_q_norm, gqa_k_norm, w_branch_mla,
           w_branch_gqa, w_out, norm_ffn2, w_ffn2_in, w_ffn2_out):
    B, S, D = x.shape
    mod = _ada(c, w_ada, b_ada).reshape(DEPTH, B, N_MOD, D)
    rope_m, rope_g = _rope_tables(S)
    lane = np.arange(2 * LANES)
    same_tile = (lane[:, None] // LANES) == (lane[None, :] // LANES)
    same_parity = _GQA_LANE_PARITY[lane % LANES][:, None] == _GQA_LANE_PARITY[lane % LANES][None, :]
    bd_m = jnp.asarray(same_tile, _BF)
    bd_g = jnp.asarray(same_tile & same_parity, _BF)
    w1_in, w1_out = w_ffn1_in.astype(_BF), w_ffn1_out.astype(_BF)
    w2_in, w2_out = w_ffn2_in.astype(_BF), w_ffn2_out.astype(_BF)
    wb_mla, wb_gqa, wo = w_branch_mla.astype(_BF), w_branch_gqa.astype(_BF), w_out.astype(_BF)
    norm_ffn1, norm_mix, norm_ffn2 = (g.reshape(DEPTH, 1, D) for g in (norm_ffn1, norm_mix, norm_ffn2))

    h = x
    for l in range(DEPTH):
        h = _ffn(h, mod, norm_ffn1, w1_in, w1_out, layer=l, mod0=0)
        wm = _pack_mix_weights(l, w_in, mla_q_a_norm, mla_w_uq, mla_kv_a_norm, mla_w_ukv,
                               mla_qk_q_norm, mla_qk_k_norm, gqa_q_norm, gqa_k_norm)
        q, k, v_t, gate = _mix_in(h, mod, norm_mix, wm, rope_m, rope_g, bd_m, bd_g, layer=l)
        o_att = _attn(q, k, v_t)
        h = _mix_out(h, mod, o_att, gate, wb_mla, wb_gqa, wo, layer=l)
        h = _ffn(h, mod, norm_ffn2, w2_in, w2_out, layer=l, mod0=6)
    return h
```

```python
import functools
import math

import jax
import jax.numpy as jnp
import numpy as np
from jax import lax
from jax.experimental import pallas as pl
from jax.experimental.pallas import tpu as pltpu

D_MODEL = 1024
DEPTH = 2
GRID_W = 64
ROPE_THETA = 10000.0
NORM_EPS = 1e-6
MLA_HEADS = 8
MLA_Q_RANK = 384
MLA_KV_RANK = 256
MLA_NOPE_DIM = 64
MLA_ROPE_DIM = 32
MLA_V_DIM = 64
MLA_QK_DIM = MLA_NOPE_DIM + MLA_ROPE_DIM
GQA_Q_HEADS = 8
GQA_KV_HEADS = 2
GQA_GROUP = GQA_Q_HEADS // GQA_KV_HEADS
GQA_HEAD_DIM = 64
D_FF = 2816
N_MOD = 9

LANES = 128
HALF = LANES // 2
ROT = 16
BF16_ROWS = 16
LOG2E = math.log2(math.e)

HEAD_V = 64
V_ROWS = HEAD_V + BF16_ROWS
N_Q = MLA_HEADS + GQA_Q_HEADS
N_KV = MLA_HEADS + GQA_KV_HEADS
N_QT = MLA_HEADS + GQA_Q_HEADS // 2
N_KT = MLA_HEADS + 2 * GQA_KV_HEADS

_MLA_LANE_DIM = np.full(LANES, -1, np.int64)
_MLA_LANE_DIM[0:ROT] = MLA_NOPE_DIM + np.arange(ROT)
_MLA_LANE_DIM[HALF:HALF + ROT] = MLA_NOPE_DIM + ROT + np.arange(ROT)
_MLA_LANE_DIM[ROT:HALF] = np.arange(HALF - ROT)
_MLA_LANE_DIM[HALF + ROT:HALF + 2 * ROT] = HALF - ROT + np.arange(ROT)
_GQA_LANE_PARITY = (np.arange(LANES) // (2 * ROT)) % 2
_GQA_LANE_DIM = (2 * ROT * ((np.arange(LANES) // ROT) % 2)
                 + ROT * (np.arange(LANES) // HALF)
                 + np.arange(LANES) % ROT)

_C_Q = 0
_C_KR = _C_Q + MLA_Q_RANK
_C_KV = _C_KR + LANES
_C_GQ = _C_KV + MLA_KV_RANK
_C_GK = _C_GQ + (GQA_Q_HEADS // 2) * LANES
_C_GATE = _C_GK + GQA_KV_HEADS * LANES
_C_GV = _C_GATE + 2 * D_MODEL
_C_END = _C_GV + GQA_KV_HEADS * GQA_HEAD_DIM

VMEM_LIMIT = 56 * 1024 * 1024

FFN_TM = 1024
FFN_TSUB = 512
FFN_TF = 256
MIX_TM = 512
ATT_TQ = 256
ATT_WIDE = 2
MAX_ROWS = 32
KEY_CHUNK = 512
_DONE = object()
OUT_TM = 512

_BF = jnp.bfloat16
_F32 = jnp.float32


def _params():
    return pltpu.CompilerParams(dimension_semantics=("arbitrary", "arbitrary"),
                                vmem_limit_bytes=VMEM_LIMIT)


def _dot(a, b):
    return jnp.dot(a, b, preferred_element_type=_F32)


def _dot_nt(a, b):
    return lax.dot_general(a, b, (((1,), (1,)), ((), ())), preferred_element_type=_F32)


def _resident(block_shape, index_map):
    return pl.BlockSpec(block_shape, index_map, pipeline_mode=pl.Buffered(1))


def _rms(x, inv_n):
    return x * lax.rsqrt(jnp.sum(x * x, axis=-1, keepdims=True) * inv_n + NORM_EPS)


def _ada_kernel(c_ref, w_ref, b_ref, o_ref):
    c = c_ref[...]
    c_act = c / (1.0 + jnp.exp(-c))
    o_ref[0] = _dot(c_act.astype(_BF), w_ref[0].astype(_BF)) + b_ref[0]


def _ada(c, w_ada, b_ada):
    L, D, N = w_ada.shape
    B = c.shape[0]
    tn = D_MODEL
    return pl.pallas_call(
        _ada_kernel,
        out_shape=jax.ShapeDtypeStruct((L, B, N), _F32),
        grid=(L, N // tn),
        in_specs=[
            pl.BlockSpec((B, D), lambda l, j: (0, 0)),
            pl.BlockSpec((1, D, tn), lambda l, j: (l, 0, j)),
            pl.BlockSpec((1, 1, tn), lambda l, j: (l, 0, j)),
        ],
        out_specs=pl.BlockSpec((1, B, tn), lambda l, j: (l, 0, j)),
        compiler_params=_params(),
        name="ada",
    )(c, w_ada, b_ada.reshape(L, 1, N))


def _ffn_kernel(h_ref, mod_ref, g_ref, win_ref, wout_ref, o_ref, act_ref, *, mod0):
    shift = mod_ref[0, 0, mod0:mod0 + 1, :]
    scale1 = (1.0 + mod_ref[0, 0, mod0 + 1:mod0 + 2, :]) * g_ref[...]
    half_gate = 0.5 * mod_ref[0, 0, mod0 + 2:mod0 + 3, :]

    def sub(r, carry):
        r0 = pl.multiple_of(r * FFN_TSUB, FFN_TSUB)
        x = h_ref[0, pl.ds(r0, FFN_TSUB), :]
        u = (_rms(x, 1.0 / D_MODEL) * scale1 + shift).astype(_BF)
        for c0 in range(0, D_FF, FFN_TF):
            a = _dot(u, win_ref[:, c0:c0 + FFN_TF])
            b = _dot(u, win_ref[:, D_FF + c0:D_FF + c0 + FFN_TF])
            act_ref[:, c0:c0 + FFN_TF] = (a / (1.0 + jnp.exp(-a)) * b).astype(_BF)
        y = _dot(act_ref[...], wout_ref[...])
        o_ref[0, pl.ds(r0, FFN_TSUB), :] = x + half_gate * y
        return carry

    lax.fori_loop(0, FFN_TM // FFN_TSUB, sub, 0)


def _ffn(h, mod, gain, w_in, w_out, *, layer, mod0):
    B, S, D = h.shape
    kern = functools.partial(_ffn_kernel, mod0=mod0)
    return pl.pallas_call(
        kern,
        out_shape=jax.ShapeDtypeStruct((B, S, D), _F32),
        grid=(B, S // FFN_TM),
        in_specs=[
            pl.BlockSpec((1, FFN_TM, D), lambda b, i: (b, i, 0)),
            pl.BlockSpec((1, 1, N_MOD, D), lambda b, i: (layer, b, 0, 0)),
            pl.BlockSpec((None, 1, D), lambda b, i: (layer, 0, 0)),
            _resident((None, D, 2 * D_FF), lambda b, i: (layer, 0, 0)),
            _resident((None, D_FF, D), lambda b, i: (layer, 0, 0)),
        ],
        out_specs=pl.BlockSpec((1, FFN_TM, D), lambda b, i: (b, i, 0)),
        scratch_shapes=[pltpu.VMEM((FFN_TSUB, D_FF), _BF)],
        compiler_params=_params(),
        name="ffn",
    )(h, mod, gain, w_in, w_out)


def _norm_rope_heads(x, bd, inv_n, gain, cos, sin, add=None):
    tiles = []
    for p0 in range(0, x.shape[1], 2 * LANES):
        xs = x[:, p0:p0 + 2 * LANES]
        if add is not None:
            xs = xs + add
        ss = _dot((xs * xs).astype(_BF), bd)
        xn = xs * lax.rsqrt(ss * inv_n + NORM_EPS)
        for j in range(2):
            xh = xn[:, j * LANES:(j + 1) * LANES] * gain
            tiles.append(xh * cos + pltpu.roll(xh, HALF, 1) * sin)
    return tiles


def _mix_in_kernel(h_ref, mod_ref, g_ref, win_ref, qan_ref, wuq_ref, kvan_ref, wuk_ref, wuv_ref,
                   qn_ref, kn_ref, gqn_ref, gkn_ref, rm_ref, rg_ref, bdm_ref, bdg_ref,
                   q_ref, k_ref, v_ref, gate_ref):
    tm = h_ref.shape[1]
    shift = mod_ref[0, 0, 3:4, :]
    scale1 = (1.0 + mod_ref[0, 0, 4:5, :]) * g_ref[...]
    u = (_rms(h_ref[0], 1.0 / D_MODEL) * scale1 + shift).astype(_BF)

    def proj(c0, c1):
        return _dot(u, win_ref[:, c0:c1])

    bd_m, bd_g = bdm_ref[...], bdg_ref[...]
    cos_m, sin_m = rm_ref[0], rm_ref[1]
    cos_g, sin_g = rg_ref[0], rg_ref[1]

    zq = proj(_C_Q, _C_KV)
    q_lat = (_rms(zq[:, :MLA_Q_RANK], 1.0 / MLA_Q_RANK) * qan_ref[...]).astype(_BF)
    k_pe = zq[:, MLA_Q_RANK:]
    for t, tile in enumerate(_norm_rope_heads(_dot(q_lat, wuq_ref[...]), bd_m, 1.0 / MLA_QK_DIM, qn_ref[...],
                                              cos_m, sin_m)):
        q_ref[0, t] = tile.astype(_BF)

    zkv = proj(_C_KV, _C_GQ)
    kv_lat = (_rms(zkv, 1.0 / MLA_KV_RANK) * kvan_ref[...]).astype(_BF)
    for t, tile in enumerate(_norm_rope_heads(_dot(kv_lat, wuk_ref[...]), bd_m, 1.0 / MLA_QK_DIM, kn_ref[...],
                                              cos_m, sin_m, add=jnp.concatenate([k_pe, k_pe], axis=1))):
        k_ref[0, t] = tile.astype(_BF)
    v_t = jnp.concatenate([_dot(kv_lat, wuv_ref[...]), proj(_C_GV, _C_END)], axis=1).T
    ones_rows = jnp.where(lax.broadcasted_iota(jnp.int32, (BF16_ROWS, tm), 0) == 0, 1.0, 0.0).astype(_BF)
    for kv in range(N_KV):
        v_ref[0, kv * V_ROWS:kv * V_ROWS + HEAD_V, :] = v_t[kv * HEAD_V:(kv + 1) * HEAD_V, :].astype(_BF)
        v_ref[0, kv * V_ROWS + HEAD_V:(kv + 1) * V_ROWS, :] = ones_rows

    for t, tile in enumerate(_norm_rope_heads(proj(_C_GQ, _C_GK), bd_g, 1.0 / GQA_HEAD_DIM, gqn_ref[...],
                                              cos_g, sin_g)):
        q_ref[0, MLA_HEADS + t] = tile.astype(_BF)
    lane_parity = lax.bitwise_and(lax.shift_right_logical(lax.broadcasted_iota(jnp.int32, (1, LANES), 1), 5), 1)
    for kv, tile in enumerate(_norm_rope_heads(proj(_C_GK, _C_GATE), bd_g, 1.0 / GQA_HEAD_DIM, gkn_ref[...],
                                               cos_g, sin_g)):
        for par in range(2):
            k_ref[0, MLA_HEADS + 2 * kv + par] = jnp.where(lane_parity == par, tile, 0.0).astype(_BF)

    zg = proj(_C_GATE, _C_GV)
    gate_ref[0] = (1.0 / (1.0 + jnp.exp(-zg))).astype(_BF)


def _mix_in(h, mod, gain, w, rope_m, rope_g, bd_m, bd_g, *, layer):
    B, S, D = h.shape
    tm = MIX_TM
    full = lambda shape: _resident(shape, lambda b, i: (0,) * len(shape))
    out_shape = (
        jax.ShapeDtypeStruct((B, N_QT, S, LANES), _BF),
        jax.ShapeDtypeStruct((B, N_KT, S, LANES), _BF),
        jax.ShapeDtypeStruct((B, N_KV * V_ROWS, S), _BF),
        jax.ShapeDtypeStruct((B, S, 2 * D), _BF),
    )
    out_specs = (
        pl.BlockSpec((1, N_QT, tm, LANES), lambda b, i: (b, 0, i, 0)),
        pl.BlockSpec((1, N_KT, tm, LANES), lambda b, i: (b, 0, i, 0)),
        pl.BlockSpec((1, N_KV * V_ROWS, tm), lambda b, i: (b, 0, i)),
        pl.BlockSpec((1, tm, 2 * D), lambda b, i: (b, i, 0)),
    )
    names = ("w_in", "q_a_norm", "w_uq", "kv_a_norm", "w_uk", "w_uv",
             "qk_q_norm", "qk_k_norm", "gqa_q_norm", "gqa_k_norm")
    in_specs = [
        pl.BlockSpec((1, tm, D), lambda b, i: (b, i, 0)),
        pl.BlockSpec((1, 1, N_MOD, D), lambda b, i: (layer, b, 0, 0)),
        pl.BlockSpec((None, 1, D), lambda b, i: (layer, 0, 0)),
        *[full(w[n].shape) for n in names],
        pl.BlockSpec((2, tm, LANES), lambda b, i: (0, i, 0)),
        pl.BlockSpec((2, tm, LANES), lambda b, i: (0, i, 0)),
        full(bd_m.shape),
        full(bd_g.shape),
    ]
    return pl.pallas_call(
        _mix_in_kernel,
        out_shape=out_shape,
        grid=(B, S // tm),
        in_specs=in_specs,
        out_specs=out_specs,
        compiler_params=_params(),
        name="mix_in",
    )(h, mod, gain, *[w[n] for n in names], rope_m, rope_g, bd_m, bd_g)


def _head_tiles(head):
    if isinstance(head, int):
        if head < MLA_HEADS:
            return head, head, head
        g = head - MLA_HEADS
        return MLA_HEADS + g // 2, MLA_HEADS + 2 * (g // GQA_GROUP) + g % 2, MLA_HEADS + g // GQA_GROUP
    g = head - MLA_HEADS
    kv = lax.shift_right_logical(g, 2)
    is_mla = head < MLA_HEADS
    return (jnp.where(is_mla, head, MLA_HEADS + lax.shift_right_logical(g, 1)),
            jnp.where(is_mla, head, MLA_HEADS + 2 * kv + lax.bitwise_and(g, 1)),
            jnp.where(is_mla, head, MLA_HEADS + kv))


def _attn_kernel(q_ref, k_ref, v_ref, o_ref, *scratch):
    s_refs, (m_ref, ot_ref) = scratch[:-2], scratch[-2:]
    n_keys = k_ref.shape[2]
    tq = s_refs[0].shape[1]
    n_units = (q_ref.shape[2] // tq) * N_Q
    chunks = [(c0, KEY_CHUNK) for c0 in range(0, n_keys, KEY_CHUNK)]

    def unit(n):
        if isinstance(n, int):
            return n % N_Q, (n // N_Q) * tq
        head = lax.bitwise_and(n, N_Q - 1)
        return head, pl.multiple_of(lax.shift_right_logical(n, N_Q.bit_length() - 1) * tq, tq)

    def scores(n, slot):
        head, q0 = unit(n)
        q_tile, k_tile, _ = _head_tiles(head)
        q = q_ref[0, q_tile, pl.ds(q0, tq), :]
        acc = None
        for c0, cn in chunks:
            s_c = _dot_nt(k_ref[0, k_tile, c0:c0 + cn, :], q)
            s_refs[slot][c0:c0 + cn, :] = s_c
            for r0 in range(0, cn, MAX_ROWS):
                blk = s_c[r0:r0 + MAX_ROWS]
                acc = blk if acc is None else jnp.maximum(acc, blk)
            yield
        m_ref[slot] = jnp.max(acc, axis=0, keepdims=True)
        yield

    def combine(n, slot):
        head, _ = unit(n)
        r0 = _head_tiles(head)[2] * V_ROWS
        c_out = head * HEAD_V
        if not isinstance(n, int):
            r0 = pl.multiple_of(r0, BF16_ROWS)
            c_out = pl.multiple_of(c_out, HEAD_V)
        m = m_ref[slot]
        o_aug = None
        for c0, cn in chunks:
            p = jnp.exp2(s_refs[slot][c0:c0 + cn, :] - m).astype(_BF)
            part = _dot(v_ref[0, pl.ds(r0, V_ROWS), c0:c0 + cn], p)
            o_aug = part if o_aug is None else o_aug + part
            yield
        ot_ref[pl.ds(c_out, HEAD_V), :] = o_aug[:HEAD_V] / o_aug[HEAD_V:HEAD_V + 1]
        yield

    def run(*stages):
        live = list(stages)
        while live:
            for g in list(live):
                if next(g, _DONE) is _DONE:
                    live.remove(g)

    def flush(q0):
        o_ref[0, pl.ds(q0, tq), :] = ot_ref[...].T.astype(_BF)

    wide = len(s_refs) // 2
    n_groups = n_units // wide

    def score_group(first, bank):
        return [scores(first + j, bank * wide + j) for j in range(wide)]

    def combine_group(first, bank):
        return [combine(first + j, bank * wide + j) for j in range(wide)]

    run(*score_group(0, 0))

    def two_groups(i, carry):
        n = 2 * wide * i
        run(*score_group(n + wide, 1), *combine_group(n, 0))
        run(*score_group(n + 2 * wide, 0), *combine_group(n + wide, 1))
        head, q0 = unit(n + 2 * wide - 1)

        @pl.when(head == N_Q - 1)
        def _():
            flush(q0)

        return carry

    lax.fori_loop(0, n_groups // 2 - 1, two_groups, 0)
    last = n_units - wide
    run(*score_group(last, 1), *combine_group(last - wide, 0))
    run(*combine_group(last, 1))
    flush(unit(n_units - 1)[1])


def _attn(q, k, v_t):
    B, _, S, _ = q.shape
    tq = ATT_TQ
    return pl.pallas_call(
        _attn_kernel,
        out_shape=jax.ShapeDtypeStruct((B, S, N_Q * HEAD_V), _BF),
        grid=(B,),
        in_specs=[
            pl.BlockSpec((1, N_QT, S, LANES), lambda b: (b, 0, 0, 0)),
            pl.BlockSpec((1, N_KT, S, LANES), lambda b: (b, 0, 0, 0)),
            pl.BlockSpec((1, N_KV * V_ROWS, S), lambda b: (b, 0, 0)),
        ],
        out_specs=pl.BlockSpec((1, S, N_Q * HEAD_V), lambda b: (b, 0, 0)),
        scratch_shapes=[
            *[pltpu.VMEM((S, tq), _F32) for _ in range(2 * ATT_WIDE)],
            pltpu.VMEM((2 * ATT_WIDE, 1, tq), _F32),
            pltpu.VMEM((N_Q * HEAD_V, tq), _F32),
        ],
        compiler_params=pltpu.CompilerParams(dimension_semantics=("arbitrary",),
                                             vmem_limit_bytes=VMEM_LIMIT),
        name="attn",
    )(q, k, v_t)


def _mix_out_kernel(h_ref, mod_ref, o_ref_in, gate_ref, wbm_ref, wbg_ref, wo_ref, o_ref):
    g2 = mod_ref[0, 0, 5:6, :]
    n_mla = MLA_HEADS * HEAD_V
    bm = _dot(o_ref_in[0, :, :n_mla], wbm_ref[...])
    bg = _dot(o_ref_in[0, :, n_mla:], wbg_ref[...])
    merged = gate_ref[0, :, :D_MODEL] * bm + gate_ref[0, :, D_MODEL:] * bg
    o_ref[0] = h_ref[0] + g2 * _dot(merged.astype(_BF), wo_ref[...])


def _mix_out(h, mod, o_att, gate, wb_mla, wb_gqa, w_out, *, layer):
    B, S, D = h.shape
    tm = OUT_TM
    row = lambda n: pl.BlockSpec((1, tm, n), lambda b, i: (b, i, 0))
    wspec = lambda a: _resident((None,) + a.shape[1:], lambda b, i: (layer, 0, 0))
    return pl.pallas_call(
        _mix_out_kernel,
        out_shape=jax.ShapeDtypeStruct((B, S, D), _F32),
        grid=(B, S // tm),
        in_specs=[
            row(D),
            pl.BlockSpec((1, 1, N_MOD, D), lambda b, i: (layer, b, 0, 0)),
            row(o_att.shape[-1]), row(2 * D),
            wspec(wb_mla), wspec(wb_gqa), wspec(w_out),
        ],
        out_specs=row(D),
        compiler_params=_params(),
        name="mix_out",
    )(h, mod, o_att, gate, wb_mla, wb_gqa, w_out)


def _rope_angles(pos, dim):
    inv = ROPE_THETA ** (-jnp.arange(0, dim, 2, dtype=_F32) / dim)
    ang = pos.astype(_F32)[:, None] * inv[None, :]
    return jnp.cos(ang), jnp.sin(ang)


def _rope_tables(S):
    t = jnp.arange(S)
    one = lambda n: jnp.ones((S, n), _F32)
    zero = lambda n: jnp.zeros((S, n), _F32)
    c, s = _rope_angles(t, MLA_ROPE_DIM)
    rope_m = jnp.stack([
        jnp.concatenate([c, one(HALF - ROT), c, one(HALF - ROT)], axis=1),
        jnp.concatenate([-s, zero(HALF - ROT), s, zero(HALF - ROT)], axis=1),
    ])
    cr, sr = _rope_angles(t // GRID_W, GQA_HEAD_DIM // 2)
    cc, sc = _rope_angles(t % GRID_W, GQA_HEAD_DIM // 2)
    sign = jnp.where(jnp.arange(LANES) < HALF, -1.0, 1.0).astype(_F32)
    rope_g = jnp.stack([
        jnp.tile(jnp.concatenate([cr, cc], axis=1), (1, LANES // (2 * ROT))),
        jnp.tile(jnp.concatenate([sr, sc], axis=1), (1, LANES // (2 * ROT))) * sign[None, :],
    ])
    return rope_m, rope_g


def _to_lanes(w, heads, dim, lane_dim):
    lead = w.shape[:-1]
    w = w.reshape(lead + (heads, dim))
    w = jnp.concatenate([w, jnp.zeros(lead + (heads, 1), w.dtype)], axis=-1)
    w = jnp.take(w, jnp.asarray(np.where(lane_dim < 0, dim, lane_dim)), axis=-1)
    return w.reshape(lead + (heads * LANES,))


def _take_cols(w, cols):
    return jnp.take(w, jnp.asarray(cols), axis=-1)


_GQ_COLS = np.concatenate([(2 * t + _GQA_LANE_PARITY) * GQA_HEAD_DIM + _GQA_LANE_DIM
                           for t in range(GQA_Q_HEADS // 2)])
_GK_COLS = np.concatenate([kv * GQA_HEAD_DIM + _GQA_LANE_DIM for kv in range(GQA_KV_HEADS)])


def _pack_mix_weights(l, w_in, q_a_norm, w_uq, kv_a_norm, w_ukv, qk_q_norm, qk_k_norm, gqa_q_norm, gqa_k_norm):
    D = D_MODEL
    wi = w_in[l]
    sizes = (MLA_Q_RANK, MLA_KV_RANK, MLA_ROPE_DIM, GQA_Q_HEADS * GQA_HEAD_DIM,
             GQA_KV_HEADS * GQA_HEAD_DIM, GQA_KV_HEADS * GQA_HEAD_DIM, 2 * D)
    parts, start = [], 0
    for s in sizes:
        parts.append(wi[:, start:start + s])
        start += s
    zq, zkv, zkr, gq, gk, gv, gate = parts
    kr_tile = _to_lanes(jnp.pad(zkr, ((0, 0), (MLA_NOPE_DIM, 0))), 1, MLA_QK_DIM, _MLA_LANE_DIM)
    w_in_p = jnp.concatenate([zq, kr_tile, zkv, _take_cols(gq, _GQ_COLS), _take_cols(gk, _GK_COLS), gate, gv],
                             axis=1)
    ukv = w_ukv[l].reshape(MLA_KV_RANK, MLA_HEADS, MLA_NOPE_DIM + MLA_V_DIM)
    uk = jnp.pad(ukv[:, :, :MLA_NOPE_DIM], ((0, 0), (0, 0), (0, MLA_ROPE_DIM))).reshape(MLA_KV_RANK, -1)
    w_uv = ukv[:, :, MLA_NOPE_DIM:].reshape(MLA_KV_RANK, -1)
    return {
        "w_in": w_in_p.astype(_BF),
        "q_a_norm": q_a_norm[l].reshape(1, -1),
        "w_uq": _to_lanes(w_uq[l], MLA_HEADS, MLA_QK_DIM, _MLA_LANE_DIM).astype(_BF),
        "kv_a_norm": kv_a_norm[l].reshape(1, -1),
        "w_uk": _to_lanes(uk, MLA_HEADS, MLA_QK_DIM, _MLA_LANE_DIM).astype(_BF),
        "w_uv": w_uv.astype(_BF),
        "qk_q_norm": _to_lanes(qk_q_norm[l] * (MLA_QK_DIM ** -0.5 * LOG2E), 1, MLA_QK_DIM, _MLA_LANE_DIM)[None],
        "qk_k_norm": _to_lanes(qk_k_norm[l], 1, MLA_QK_DIM, _MLA_LANE_DIM)[None],
        "gqa_q_norm": _take_cols(gqa_q_norm[l] * (GQA_HEAD_DIM ** -0.5 * LOG2E), _GQA_LANE_DIM)[None],
        "gqa_k_norm": _take_cols(gqa_k_norm[l], _GQA_LANE_DIM)[None],
    }


def kernel(x, c, w_ada, b_ada, norm_ffn1, w_ffn1_in, w_ffn1_out, norm_mix, w_in, mla_q_a_norm, mla_w_uq,
           mla_kv_a_norm, mla_w_ukv, mla_qk_q_norm, mla_qk_k_norm, gqa_q_norm, gqa_k_norm, w_branch_mla,
           w_branch_gqa, w_out, norm_ffn2, w_ffn2_in, w_ffn2_out):
    B, S, D = x.shape
    mod = _ada(c, w_ada, b_ada).reshape(DEPTH, B, N_MOD, D)
    rope_m, rope_g = _rope_tables(S)
    lane = np.arange(2 * LANES)
    same_tile = (lane[:, None] // LANES) == (lane[None, :] // LANES)
    same_parity = _GQA_LANE_PARITY[lane % LANES][:, None] == _GQA_LANE_PARITY[lane % LANES][None, :]
    bd_m = jnp.asarray(same_tile, _BF)
    bd_g = jnp.asarray(same_tile & same_parity, _BF)
    w1_in, w1_out = w_ffn1_in.astype(_BF), w_ffn1_out.astype(_BF)
    w2_in, w2_out = w_ffn2_in.astype(_BF), w_ffn2_out.astype(_BF)
    wb_mla, wb_gqa, wo = w_branch_mla.astype(_BF), w_branch_gqa.astype(_BF), w_out.astype(_BF)
    norm_ffn1, norm_mix, norm_ffn2 = (g.reshape(DEPTH, 1, D) for g in (norm_ffn1, norm_mix, norm_ffn2))

    h = x
    for l in range(DEPTH):
        h = _ffn(h, mod, norm_ffn1, w1_in, w1_out, layer=l, mod0=0)
        wm = _pack_mix_weights(l, w_in, mla_q_a_norm, mla_w_uq, mla_kv_a_norm, mla_w_ukv,
                               mla_qk_q_norm, mla_qk_k_norm, gqa_q_norm, gqa_k_norm)
        q, k, v_t, gate = _mix_in(h, mod, norm_mix, wm, rope_m, rope_g, bd_m, bd_g, layer=l)
        o_att = _attn(q, k, v_t)
        h = _mix_out(h, mod, o_att, gate, wb_mla, wb_gqa, wo, layer=l)
        h = _ffn(h, mod, norm_ffn2, w2_in, w2_out, layer=l, mod0=6)
    return h
```

```python
import functools
import math

import jax
import jax.numpy as jnp
import numpy as np
from jax import lax
from jax.experimental import pallas as pl
from jax.experimental.pallas import tpu as pltpu

D_MODEL = 1024
DEPTH = 2
GRID_W = 64
ROPE_THETA = 10000.0
NORM_EPS = 1e-6
MLA_HEADS = 8
MLA_Q_RANK = 384
MLA_KV_RANK = 256
MLA_NOPE_DIM = 64
MLA_ROPE_DIM = 32
MLA_V_DIM = 64
MLA_QK_DIM = MLA_NOPE_DIM + MLA_ROPE_DIM
GQA_Q_HEADS = 8
GQA_KV_HEADS = 2
GQA_GROUP = GQA_Q_HEADS // GQA_KV_HEADS
GQA_HEAD_DIM = 64
D_FF = 2816
N_MOD = 9

LANES = 128
HALF = LANES // 2
ROT = 16
BF16_ROWS = 16
LOG2E = math.log2(math.e)

HEAD_V = 64
V_ROWS = HEAD_V + BF16_ROWS
N_Q = MLA_HEADS + GQA_Q_HEADS
N_KV = MLA_HEADS + GQA_KV_HEADS
N_QT = MLA_HEADS + GQA_Q_HEADS // 2
N_KT = MLA_HEADS + 2 * GQA_KV_HEADS

_MLA_LANE_DIM = np.full(LANES, -1, np.int64)
_MLA_LANE_DIM[0:ROT] = MLA_NOPE_DIM + np.arange(ROT)
_MLA_LANE_DIM[HALF:HALF + ROT] = MLA_NOPE_DIM + ROT + np.arange(ROT)
_MLA_LANE_DIM[ROT:HALF] = np.arange(HALF - ROT)
_MLA_LANE_DIM[HALF + ROT:HALF + 2 * ROT] = HALF - ROT + np.arange(ROT)
_GQA_LANE_PARITY = (np.arange(LANES) // (2 * ROT)) % 2
_GQA_LANE_DIM = (2 * ROT * ((np.arange(LANES) // ROT) % 2)
                 + ROT * (np.arange(LANES) // HALF)
                 + np.arange(LANES) % ROT)

_C_Q = 0
_C_KR = _C_Q + MLA_Q_RANK
_C_KV = _C_KR + LANES
_C_GQ = _C_KV + MLA_KV_RANK
_C_GK = _C_GQ + (GQA_Q_HEADS // 2) * LANES
_C_GATE = _C_GK + GQA_KV_HEADS * LANES
_C_GV = _C_GATE + 2 * D_MODEL
_C_END = _C_GV + GQA_KV_HEADS * GQA_HEAD_DIM

VMEM_LIMIT = 56 * 1024 * 1024

FFN_TM = 1024
FFN_TSUB = 1024
FFN_TF = 256
MIX_TM = 512
ATT_TQ = 256
ATT_WIDE = 2
MAX_ROWS = 32
KEY_CHUNK = 2048
_DONE = object()
OUT_TM = 512

_BF = jnp.bfloat16
_F32 = jnp.float32


def _params():
    return pltpu.CompilerParams(dimension_semantics=("arbitrary", "arbitrary"),
                                vmem_limit_bytes=VMEM_LIMIT)


def _dot(a, b):
    return jnp.dot(a, b, preferred_element_type=_F32)


def _dot_nt(a, b):
    return lax.dot_general(a, b, (((1,), (1,)), ((), ())), preferred_element_type=_F32)


def _resident(block_shape, index_map):
    return pl.BlockSpec(block_shape, index_map, pipeline_mode=pl.Buffered(1))


def _rms(x, inv_n):
    return x * lax.rsqrt(jnp.sum(x * x, axis=-1, keepdims=True) * inv_n + NORM_EPS)


def _ada_kernel(c_ref, w_ref, b_ref, o_ref):
    c = c_ref[...]
    c_act = c / (1.0 + jnp.exp(-c))
    o_ref[0] = _dot(c_act.astype(_BF), w_ref[0].astype(_BF)) + b_ref[0]


def _ada(c, w_ada, b_ada):
    L, D, N = w_ada.shape
    B = c.shape[0]
    tn = D_MODEL
    return pl.pallas_call(
        _ada_kernel,
        out_shape=jax.ShapeDtypeStruct((L, B, N), _F32),
        grid=(L, N // tn),
        in_specs=[
            pl.BlockSpec((B, D), lambda l, j: (0, 0)),
            pl.BlockSpec((1, D, tn), lambda l, j: (l, 0, j)),
            pl.BlockSpec((1, 1, tn), lambda l, j: (l, 0, j)),
        ],
        out_specs=pl.BlockSpec((1, B, tn), lambda l, j: (l, 0, j)),
        compiler_params=_params(),
        name="ada",
    )(c, w_ada, b_ada.reshape(L, 1, N))


def _ffn_kernel(h_ref, mod_ref, g_ref, win_ref, wout_ref, o_ref, act_ref, *, mod0):
    shift = mod_ref[0, 0, mod0:mod0 + 1, :]
    scale1 = (1.0 + mod_ref[0, 0, mod0 + 1:mod0 + 2, :]) * g_ref[...]
    half_gate = 0.5 * mod_ref[0, 0, mod0 + 2:mod0 + 3, :]

    def sub(r, carry):
        r0 = pl.multiple_of(r * FFN_TSUB, FFN_TSUB)
        x = h_ref[0, pl.ds(r0, FFN_TSUB), :]
        u = (_rms(x, 1.0 / D_MODEL) * scale1 + shift).astype(_BF)
        for c0 in range(0, D_FF, FFN_TF):
            a = _dot(u, win_ref[:, c0:c0 + FFN_TF])
            b = _dot(u, win_ref[:, D_FF + c0:D_FF + c0 + FFN_TF])
            act_ref[:, c0:c0 + FFN_TF] = (a / (1.0 + jnp.exp(-a)) * b).astype(_BF)
        y = _dot(act_ref[...], wout_ref[...])
        o_ref[0, pl.ds(r0, FFN_TSUB), :] = x + half_gate * y
        return carry

    lax.fori_loop(0, FFN_TM // FFN_TSUB, sub, 0)


def _ffn(h, mod, gain, w_in, w_out, *, layer, mod0):
    B, S, D = h.shape
    kern = functools.partial(_ffn_kernel, mod0=mod0)
    return pl.pallas_call(
        kern,
        out_shape=jax.ShapeDtypeStruct((B, S, D), _F32),
        grid=(B, S // FFN_TM),
        in_specs=[
            pl.BlockSpec((1, FFN_TM, D), lambda b, i: (b, i, 0)),
            pl.BlockSpec((1, 1, N_MOD, D), lambda b, i: (layer, b, 0, 0)),
            pl.BlockSpec((None, 1, D), lambda b, i: (layer, 0, 0)),
            _resident((None, D, 2 * D_FF), lambda b, i: (layer, 0, 0)),
            _resident((None, D_FF, D), lambda b, i: (layer, 0, 0)),
        ],
        out_specs=pl.BlockSpec((1, FFN_TM, D), lambda b, i: (b, i, 0)),
        scratch_shapes=[pltpu.VMEM((FFN_TSUB, D_FF), _BF)],
        compiler_params=_params(),
        name="ffn",
    )(h, mod, gain, w_in, w_out)


def _norm_rope_heads(x, bd, inv_n, gain, cos, sin, add=None):
    tiles = []
    for p0 in range(0, x.shape[1], 2 * LANES):
        xs = x[:, p0:p0 + 2 * LANES]
        if add is not None:
            xs = xs + add
        ss = _dot((xs * xs).astype(_BF), bd)
        xn = xs * lax.rsqrt(ss * inv_n + NORM_EPS)
        for j in range(2):
            xh = xn[:, j * LANES:(j + 1) * LANES] * gain
            tiles.append(xh * cos + pltpu.roll(xh, HALF, 1) * sin)
    return tiles


def _mix_in_kernel(h_ref, mod_ref, g_ref, win_ref, qan_ref, wuq_ref, kvan_ref, wuk_ref, wuv_ref,
                   qn_ref, kn_ref, gqn_ref, gkn_ref, rm_ref, rg_ref, bdm_ref, bdg_ref,
                   q_ref, k_ref, v_ref, gate_ref):
    tm = h_ref.shape[1]
    shift = mod_ref[0, 0, 3:4, :]
    scale1 = (1.0 + mod_ref[0, 0, 4:5, :]) * g_ref[...]
    u = (_rms(h_ref[0], 1.0 / D_MODEL) * scale1 + shift).astype(_BF)

    def proj(c0, c1):
        return _dot(u, win_ref[:, c0:c1])

    bd_m, bd_g = bdm_ref[...], bdg_ref[...]
    cos_m, sin_m = rm_ref[0], rm_ref[1]
    cos_g, sin_g = rg_ref[0], rg_ref[1]

    zq = proj(_C_Q, _C_KV)
    q_lat = (_rms(zq[:, :MLA_Q_RANK], 1.0 / MLA_Q_RANK) * qan_ref[...]).astype(_BF)
    k_pe = zq[:, MLA_Q_RANK:]
    for t, tile in enumerate(_norm_rope_heads(_dot(q_lat, wuq_ref[...]), bd_m, 1.0 / MLA_QK_DIM, qn_ref[...],
                                              cos_m, sin_m)):
        q_ref[0, t] = tile.astype(_BF)

    zkv = proj(_C_KV, _C_GQ)
    kv_lat = (_rms(zkv, 1.0 / MLA_KV_RANK) * kvan_ref[...]).astype(_BF)
    for t, tile in enumerate(_norm_rope_heads(_dot(kv_lat, wuk_ref[...]), bd_m, 1.0 / MLA_QK_DIM, kn_ref[...],
                                              cos_m, sin_m, add=jnp.concatenate([k_pe, k_pe], axis=1))):
        k_ref[0, t] = tile.astype(_BF)
    v_t = jnp.concatenate([_dot(kv_lat, wuv_ref[...]), proj(_C_GV, _C_END)], axis=1).T
    ones_rows = jnp.where(lax.broadcasted_iota(jnp.int32, (BF16_ROWS, tm), 0) == 0, 1.0, 0.0).astype(_BF)
    for kv in range(N_KV):
        v_ref[0, kv * V_ROWS:kv * V_ROWS + HEAD_V, :] = v_t[kv * HEAD_V:(kv + 1) * HEAD_V, :].astype(_BF)
        v_ref[0, kv * V_ROWS + HEAD_V:(kv + 1) * V_ROWS, :] = ones_rows

    for t, tile in enumerate(_norm_rope_heads(proj(_C_GQ, _C_GK), bd_g, 1.0 / GQA_HEAD_DIM, gqn_ref[...],
                                              cos_g, sin_g)):
        q_ref[0, MLA_HEADS + t] = tile.astype(_BF)
    lane_parity = lax.bitwise_and(lax.shift_right_logical(lax.broadcasted_iota(jnp.int32, (1, LANES), 1), 5), 1)
    for kv, tile in enumerate(_norm_rope_heads(proj(_C_GK, _C_GATE), bd_g, 1.0 / GQA_HEAD_DIM, gkn_ref[...],
                                               cos_g, sin_g)):
        for par in range(2):
            k_ref[0, MLA_HEADS + 2 * kv + par] = jnp.where(lane_parity == par, tile, 0.0).astype(_BF)

    zg = proj(_C_GATE, _C_GV)
    gate_ref[0] = (1.0 / (1.0 + jnp.exp(-zg))).astype(_BF)


def _mix_in(h, mod, gain, w, rope_m, rope_g, bd_m, bd_g, *, layer):
    B, S, D = h.shape
    tm = MIX_TM
    full = lambda shape: _resident(shape, lambda b, i: (0,) * len(shape))
    out_shape = (
        jax.ShapeDtypeStruct((B, N_QT, S, LANES), _BF),
        jax.ShapeDtypeStruct((B, N_KT, S, LANES), _BF),
        jax.ShapeDtypeStruct((B, N_KV * V_ROWS, S), _BF),
        jax.ShapeDtypeStruct((B, S, 2 * D), _BF),
    )
    out_specs = (
        pl.BlockSpec((1, N_QT, tm, LANES), lambda b, i: (b, 0, i, 0)),
        pl.BlockSpec((1, N_KT, tm, LANES), lambda b, i: (b, 0, i, 0)),
        pl.BlockSpec((1, N_KV * V_ROWS, tm), lambda b, i: (b, 0, i)),
        pl.BlockSpec((1, tm, 2 * D), lambda b, i: (b, i, 0)),
    )
    names = ("w_in", "q_a_norm", "w_uq", "kv_a_norm", "w_uk", "w_uv",
             "qk_q_norm", "qk_k_norm", "gqa_q_norm", "gqa_k_norm")
    in_specs = [
        pl.BlockSpec((1, tm, D), lambda b, i: (b, i, 0)),
        pl.BlockSpec((1, 1, N_MOD, D), lambda b, i: (layer, b, 0, 0)),
        pl.BlockSpec((None, 1, D), lambda b, i: (layer, 0, 0)),
        *[full(w[n].shape) for n in names],
        pl.BlockSpec((2, tm, LANES), lambda b, i: (0, i, 0)),
        pl.BlockSpec((2, tm, LANES), lambda b, i: (0, i, 0)),
        full(bd_m.shape),
        full(bd_g.shape),
    ]
    return pl.pallas_call(
        _mix_in_kernel,
        out_shape=out_shape,
        grid=(B, S // tm),
        in_specs=in_specs,
        out_specs=out_specs,
        compiler_params=_params(),
        name="mix_in",
    )(h, mod, gain, *[w[n] for n in names], rope_m, rope_g, bd_m, bd_g)


def _head_tiles(head):
    if isinstance(head, int):
        if head < MLA_HEADS:
            return head, head, head
        g = head - MLA_HEADS
        return MLA_HEADS + g // 2, MLA_HEADS + 2 * (g // GQA_GROUP) + g % 2, MLA_HEADS + g // GQA_GROUP
    g = head - MLA_HEADS
    kv = lax.shift_right_logical(g, 2)
    is_mla = head < MLA_HEADS
    return (jnp.where(is_mla, head, MLA_HEADS + lax.shift_right_logical(g, 1)),
            jnp.where(is_mla, head, MLA_HEADS + 2 * kv + lax.bitwise_and(g, 1)),
            jnp.where(is_mla, head, MLA_HEADS + kv))


def _attn_kernel(q_ref, k_ref, v_ref, o_ref, *scratch):
    s_refs, (m_ref, ot_ref) = scratch[:-2], scratch[-2:]
    n_keys = k_ref.shape[2]
    tq = s_refs[0].shape[1]
    n_units = (q_ref.shape[2] // tq) * N_Q
    chunks = [(c0, KEY_CHUNK) for c0 in range(0, n_keys, KEY_CHUNK)]

    def unit(n):
        if isinstance(n, int):
            return n % N_Q, (n // N_Q) * tq
        head = lax.bitwise_and(n, N_Q - 1)
        return head, pl.multiple_of(lax.shift_right_logical(n, N_Q.bit_length() - 1) * tq, tq)

    def scores(n, slot):
        head, q0 = unit(n)
        q_tile, k_tile, _ = _head_tiles(head)
        q = q_ref[0, q_tile, pl.ds(q0, tq), :]
        acc = None
        for c0, cn in chunks:
            s_c = _dot_nt(k_ref[0, k_tile, c0:c0 + cn, :], q)
            s_refs[slot][c0:c0 + cn, :] = s_c
            for r0 in range(0, cn, MAX_ROWS):
                blk = s_c[r0:r0 + MAX_ROWS]
                acc = blk if acc is None else jnp.maximum(acc, blk)
            yield
        m_ref[slot] = jnp.max(acc, axis=0, keepdims=True)
        yield

    def combine(n, slot):
        head, _ = unit(n)
        r0 = _head_tiles(head)[2] * V_ROWS
        c_out = head * HEAD_V
        if not isinstance(n, int):
            r0 = pl.multiple_of(r0, BF16_ROWS)
            c_out = pl.multiple_of(c_out, HEAD_V)
        m = m_ref[slot]
        o_aug = None
        for c0, cn in chunks:
            p = jnp.exp2(s_refs[slot][c0:c0 + cn, :] - m).astype(_BF)
            part = _dot(v_ref[0, pl.ds(r0, V_ROWS), c0:c0 + cn], p)
            o_aug = part if o_aug is None else o_aug + part
            yield
        ot_ref[pl.ds(c_out, HEAD_V), :] = o_aug[:HEAD_V] / o_aug[HEAD_V:HEAD_V + 1]
        yield

    def run(*stages):
        live = list(stages)
        while live:
            for g in list(live):
                if next(g, _DONE) is _DONE:
                    live.remove(g)

    def flush(q0):
        o_ref[0, pl.ds(q0, tq), :] = ot_ref[...].T.astype(_BF)

    wide = len(s_refs) // 2
    n_groups = n_units // wide

    def score_group(first, bank):
        return [scores(first + j, bank * wide + j) for j in range(wide)]

    def combine_group(first, bank):
        return [combine(first + j, bank * wide + j) for j in range(wide)]

    run(*score_group(0, 0))

    def two_groups(i, carry):
        n = 2 * wide * i
        run(*score_group(n + wide, 1), *combine_group(n, 0))
        run(*score_group(n + 2 * wide, 0), *combine_group(n + wide, 1))
        head, q0 = unit(n + 2 * wide - 1)

        @pl.when(head == N_Q - 1)
        def _():
            flush(q0)

        return carry

    lax.fori_loop(0, n_groups // 2 - 1, two_groups, 0)
    last = n_units - wide
    run(*score_group(last, 1), *combine_group(last - wide, 0))
    run(*combine_group(last, 1))
    flush(unit(n_units - 1)[1])


def _attn(q, k, v_t):
    B, _, S, _ = q.shape
    tq = ATT_TQ
    return pl.pallas_call(
        _attn_kernel,
        out_shape=jax.ShapeDtypeStruct((B, S, N_Q * HEAD_V), _BF),
        grid=(B,),
        in_specs=[
            pl.BlockSpec((1, N_QT, S, LANES), lambda b: (b, 0, 0, 0)),
            pl.BlockSpec((1, N_KT, S, LANES), lambda b: (b, 0, 0, 0)),
            pl.BlockSpec((1, N_KV * V_ROWS, S), lambda b: (b, 0, 0)),
        ],
        out_specs=pl.BlockSpec((1, S, N_Q * HEAD_V), lambda b: (b, 0, 0)),
        scratch_shapes=[
            *[pltpu.VMEM((S, tq), _F32) for _ in range(2 * ATT_WIDE)],
            pltpu.VMEM((2 * ATT_WIDE, 1, tq), _F32),
            pltpu.VMEM((N_Q * HEAD_V, tq), _F32),
        ],
        compiler_params=pltpu.CompilerParams(dimension_semantics=("arbitrary",),
                                             vmem_limit_bytes=VMEM_LIMIT),
        name="attn",
    )(q, k, v_t)


def _mix_out_kernel(h_ref, mod_ref, o_ref_in, gate_ref, wbm_ref, wbg_ref, wo_ref, o_ref):
    g2 = mod_ref[0, 0, 5:6, :]
    n_mla = MLA_HEADS * HEAD_V
    bm = _dot(o_ref_in[0, :, :n_mla], wbm_ref[...])
    bg = _dot(o_ref_in[0, :, n_mla:], wbg_ref[...])
    merged = gate_ref[0, :, :D_MODEL] * bm + gate_ref[0, :, D_MODEL:] * bg
    o_ref[0] = h_ref[0] + g2 * _dot(merged.astype(_BF), wo_ref[...])


def _mix_out(h, mod, o_att, gate, wb_mla, wb_gqa, w_out, *, layer):
    B, S, D = h.shape
    tm = OUT_TM
    row = lambda n: pl.BlockSpec((1, tm, n), lambda b, i: (b, i, 0))
    wspec = lambda a: _resident((None,) + a.shape[1:], lambda b, i: (layer, 0, 0))
    return pl.pallas_call(
        _mix_out_kernel,
        out_shape=jax.ShapeDtypeStruct((B, S, D), _F32),
        grid=(B, S // tm),
        in_specs=[
            row(D),
            pl.BlockSpec((1, 1, N_MOD, D), lambda b, i: (layer, b, 0, 0)),
            row(o_att.shape[-1]), row(2 * D),
            wspec(wb_mla), wspec(wb_gqa), wspec(w_out),
        ],
        out_specs=row(D),
        compiler_params=_params(),
        name="mix_out",
    )(h, mod, o_att, gate, wb_mla, wb_gqa, w_out)


def _rope_angles(pos, dim):
    inv = ROPE_THETA ** (-jnp.arange(0, dim, 2, dtype=_F32) / dim)
    ang = pos.astype(_F32)[:, None] * inv[None, :]
    return jnp.cos(ang), jnp.sin(ang)


def _rope_tables(S):
    t = jnp.arange(S)
    one = lambda n: jnp.ones((S, n), _F32)
    zero = lambda n: jnp.zeros((S, n), _F32)
    c, s = _rope_angles(t, MLA_ROPE_DIM)
    rope_m = jnp.stack([
        jnp.concatenate([c, one(HALF - ROT), c, one(HALF - ROT)], axis=1),
        jnp.concatenate([-s, zero(HALF - ROT), s, zero(HALF - ROT)], axis=1),
    ])
    cr, sr = _rope_angles(t // GRID_W, GQA_HEAD_DIM // 2)
    cc, sc = _rope_angles(t % GRID_W, GQA_HEAD_DIM // 2)
    sign = jnp.where(jnp.arange(LANES) < HALF, -1.0, 1.0).astype(_F32)
    rope_g = jnp.stack([
        jnp.tile(jnp.concatenate([cr, cc], axis=1), (1, LANES // (2 * ROT))),
        jnp.tile(jnp.concatenate([sr, sc], axis=1), (1, LANES // (2 * ROT))) * sign[None, :],
    ])
    return rope_m, rope_g


def _to_lanes(w, heads, dim, lane_dim):
    lead = w.shape[:-1]
    w = w.reshape(lead + (heads, dim))
    w = jnp.concatenate([w, jnp.zeros(lead + (heads, 1), w.dtype)], axis=-1)
    w = jnp.take(w, jnp.asarray(np.where(lane_dim < 0, dim, lane_dim)), axis=-1)
    return w.reshape(lead + (heads * LANES,))


def _take_cols(w, cols):
    return jnp.take(w, jnp.asarray(cols), axis=-1)


_GQ_COLS = np.concatenate([(2 * t + _GQA_LANE_PARITY) * GQA_HEAD_DIM + _GQA_LANE_DIM
                           for t in range(GQA_Q_HEADS // 2)])
_GK_COLS = np.concatenate([kv * GQA_HEAD_DIM + _GQA_LANE_DIM for kv in range(GQA_KV_HEADS)])


def _pack_mix_weights(l, w_in, q_a_norm, w_uq, kv_a_norm, w_ukv, qk_q_norm, qk_k_norm, gqa_q_norm, gqa_k_norm):
    D = D_MODEL
    wi = w_in[l]
    sizes = (MLA_Q_RANK, MLA_KV_RANK, MLA_ROPE_DIM, GQA_Q_HEADS * GQA_HEAD_DIM,
             GQA_KV_HEADS * GQA_HEAD_DIM, GQA_KV_HEADS * GQA_HEAD_DIM, 2 * D)
    parts, start = [], 0
    for s in sizes:
        parts.append(wi[:, start:start + s])
        start += s
    zq, zkv, zkr, gq, gk, gv, gate = parts
    kr_tile = _to_lanes(jnp.pad(zkr, ((0, 0), (MLA_NOPE_DIM, 0))), 1, MLA_QK_DIM, _MLA_LANE_DIM)
    w_in_p = jnp.concatenate([zq, kr_tile, zkv, _take_cols(gq, _GQ_COLS), _take_cols(gk, _GK_COLS), gate, gv],
                             axis=1)
    ukv = w_ukv[l].reshape(MLA_KV_RANK, MLA_HEADS, MLA_NOPE_DIM + MLA_V_DIM)
    uk = jnp.pad(ukv[:, :, :MLA_NOPE_DIM], ((0, 0), (0, 0), (0, MLA_ROPE_DIM))).reshape(MLA_KV_RANK, -1)
    w_uv = ukv[:, :, MLA_NOPE_DIM:].reshape(MLA_KV_RANK, -1)
    return {
        "w_in": w_in_p.astype(_BF),
        "q_a_norm": q_a_norm[l].reshape(1, -1),
        "w_uq": _to_lanes(w_uq[l], MLA_HEADS, MLA_QK_DIM, _MLA_LANE_DIM).astype(_BF),
        "kv_a_norm": kv_a_norm[l].reshape(1, -1),
        "w_uk": _to_lanes(uk, MLA_HEADS, MLA_QK_DIM, _MLA_LANE_DIM).astype(_BF),
        "w_uv": w_uv.astype(_BF),
        "qk_q_norm": _to_lanes(qk_q_norm[l] * (MLA_QK_DIM ** -0.5 * LOG2E), 1, MLA_QK_DIM, _MLA_LANE_DIM)[None],
        "qk_k_norm": _to_lanes(qk_k_norm[l], 1, MLA_QK_DIM, _MLA_LANE_DIM)[None],
        "gqa_q_norm": _take_cols(gqa_q_norm[l] * (GQA_HEAD_DIM ** -0.5 * LOG2E), _GQA_LANE_DIM)[None],
        "gqa_k_norm": _take_cols(gqa_k_norm[l], _GQA_LANE_DIM)[None],
    }


def kernel(x, c, w_ada, b_ada, norm_ffn1, w_ffn1_in, w_ffn1_out, norm_mix, w_in, mla_q_a_norm, mla_w_uq,
           mla_kv_a_norm, mla_w_ukv, mla_qk_q_norm, mla_qk_k_norm, gqa_q_norm, gqa_k_norm, w_branch_mla,
           w_branch_gqa, w_out, norm_ffn2, w_ffn2_in, w_ffn2_out):
    B, S, D = x.shape
    mod = _ada(c, w_ada, b_ada).reshape(DEPTH, B, N_MOD, D)
    rope_m, rope_g = _rope_tables(S)
    lane = np.arange(2 * LANES)
    same_tile = (lane[:, None] // LANES) == (lane[None, :] // LANES)
    same_parity = _GQA_LANE_PARITY[lane % LANES][:, None] == _GQA_LANE_PARITY[lane % LANES][None, :]
    bd_m = jnp.asarray(same_tile, _BF)
    bd_g = jnp.asarray(same_tile & same_parity, _BF)
    w1_in, w1_out = w_ffn1_in.astype(_BF), w_ffn1_out.astype(_BF)
    w2_in, w2_out = w_ffn2_in.astype(_BF), w_ffn2_out.astype(_BF)
    wb_mla, wb_gqa, wo = w_branch_mla.astype(_BF), w_branch_gqa.astype(_BF), w_out.astype(_BF)
    norm_ffn1, norm_mix, norm_ffn2 = (g.reshape(DEPTH, 1, D) for g in (norm_ffn1, norm_mix, norm_ffn2))

    h = x
    for l in range(DEPTH):
        h = _ffn(h, mod, norm_ffn1, w1_in, w1_out, layer=l, mod0=0)
        wm = _pack_mix_weights(l, w_in, mla_q_a_norm, mla_w_uq, mla_kv_a_norm, mla_w_ukv,
                               mla_qk_q_norm, mla_qk_k_norm, gqa_q_norm, gqa_k_norm)
        q, k, v_t, gate = _mix_in(h, mod, norm_mix, wm, rope_m, rope_g, bd_m, bd_g, layer=l)
        o_att = _attn(q, k, v_t)
        h = _mix_out(h, mod, o_att, gate, wb_mla, wb_gqa, wo, layer=l)
        h = _ffn(h, mod, norm_ffn2, w2_in, w2_out, layer=l, mod0=6)
    return h
```

```python
import functools
import math

import jax
import jax.numpy as jnp
import numpy as np
from jax import lax
from jax.experimental import pallas as pl
from jax.experimental.pallas import tpu as pltpu

D_MODEL = 1024
DEPTH = 2
GRID_W = 64
ROPE_THETA = 10000.0
NORM_EPS = 1e-6
MLA_HEADS = 8
MLA_Q_RANK = 384
MLA_KV_RANK = 256
MLA_NOPE_DIM = 64
MLA_ROPE_DIM = 32
MLA_V_DIM = 64
MLA_QK_DIM = MLA_NOPE_DIM + MLA_ROPE_DIM
GQA_Q_HEADS = 8
GQA_KV_HEADS = 2
GQA_GROUP = GQA_Q_HEADS // GQA_KV_HEADS
GQA_HEAD_DIM = 64
D_FF = 2816
N_MOD = 9

LANES = 128
HALF = LANES // 2
ROT = 16
BF16_ROWS = 16
LOG2E = math.log2(math.e)

HEAD_V = 64
V_ROWS = HEAD_V + BF16_ROWS
N_Q = MLA_HEADS + GQA_Q_HEADS
N_KV = MLA_HEADS + GQA_KV_HEADS
N_QT = MLA_HEADS + GQA_Q_HEADS // 2
N_KT = MLA_HEADS + 2 * GQA_KV_HEADS

_MLA_LANE_DIM = np.full(LANES, -1, np.int64)
_MLA_LANE_DIM[0:ROT] = MLA_NOPE_DIM + np.arange(ROT)
_MLA_LANE_DIM[HALF:HALF + ROT] = MLA_NOPE_DIM + ROT + np.arange(ROT)
_MLA_LANE_DIM[ROT:HALF] = np.arange(HALF - ROT)
_MLA_LANE_DIM[HALF + ROT:HALF + 2 * ROT] = HALF - ROT + np.arange(ROT)
_GQA_LANE_PARITY = (np.arange(LANES) // (2 * ROT)) % 2
_GQA_LANE_DIM = (2 * ROT * ((np.arange(LANES) // ROT) % 2)
                 + ROT * (np.arange(LANES) // HALF)
                 + np.arange(LANES) % ROT)

_C_Q = 0
_C_KR = _C_Q + MLA_Q_RANK
_C_KV = _C_KR + LANES
_C_GQ = _C_KV + MLA_KV_RANK
_C_GK = _C_GQ + (GQA_Q_HEADS // 2) * LANES
_C_GATE = _C_GK + GQA_KV_HEADS * LANES
_C_GV = _C_GATE + 2 * D_MODEL
_C_END = _C_GV + GQA_KV_HEADS * GQA_HEAD_DIM

VMEM_LIMIT = 56 * 1024 * 1024

FFN_TM = 1024
FFN_TSUB = 1024
FFN_TF = 256
MIX_TM = 1024
ATT_TQ = 256
ATT_WIDE = 2
MAX_ROWS = 32
KEY_CHUNK = 2048
_DONE = object()
OUT_TM = 1024

_BF = jnp.bfloat16
_F32 = jnp.float32


def _params():
    return pltpu.CompilerParams(dimension_semantics=("arbitrary", "arbitrary"),
                                vmem_limit_bytes=VMEM_LIMIT)


def _dot(a, b):
    return jnp.dot(a, b, preferred_element_type=_F32)


def _dot_nt(a, b):
    return lax.dot_general(a, b, (((1,), (1,)), ((), ())), preferred_element_type=_F32)


def _resident(block_shape, index_map):
    return pl.BlockSpec(block_shape, index_map, pipeline_mode=pl.Buffered(1))


def _rms(x, inv_n):
    return x * lax.rsqrt(jnp.sum(x * x, axis=-1, keepdims=True) * inv_n + NORM_EPS)


def _ada_kernel(c_ref, w_ref, b_ref, o_ref):
    c = c_ref[...]
    c_act = c / (1.0 + jnp.exp(-c))
    o_ref[0] = _dot(c_act.astype(_BF), w_ref[0].astype(_BF)) + b_ref[0]


def _ada(c, w_ada, b_ada):
    L, D, N = w_ada.shape
    B = c.shape[0]
    tn = D_MODEL
    return pl.pallas_call(
        _ada_kernel,
        out_shape=jax.ShapeDtypeStruct((L, B, N), _F32),
        grid=(L, N // tn),
        in_specs=[
            pl.BlockSpec((B, D), lambda l, j: (0, 0)),
            pl.BlockSpec((1, D, tn), lambda l, j: (l, 0, j)),
            pl.BlockSpec((1, 1, tn), lambda l, j: (l, 0, j)),
        ],
        out_specs=pl.BlockSpec((1, B, tn), lambda l, j: (l, 0, j)),
        compiler_params=_params(),
        name="ada",
    )(c, w_ada, b_ada.reshape(L, 1, N))


def _ffn_kernel(h_ref, mod_ref, g_ref, win_ref, wout_ref, o_ref, act_ref, *, mod0):
    shift = mod_ref[0, 0, mod0:mod0 + 1, :]
    scale1 = (1.0 + mod_ref[0, 0, mod0 + 1:mod0 + 2, :]) * g_ref[...]
    half_gate = 0.5 * mod_ref[0, 0, mod0 + 2:mod0 + 3, :]

    def sub(r, carry):
        r0 = pl.multiple_of(r * FFN_TSUB, FFN_TSUB)
        x = h_ref[0, pl.ds(r0, FFN_TSUB), :]
        u = (_rms(x, 1.0 / D_MODEL) * scale1 + shift).astype(_BF)
        for c0 in range(0, D_FF, FFN_TF):
            a = _dot(u, win_ref[:, c0:c0 + FFN_TF])
            b = _dot(u, win_ref[:, D_FF + c0:D_FF + c0 + FFN_TF])
            act_ref[:, c0:c0 + FFN_TF] = (a / (1.0 + jnp.exp(-a)) * b).astype(_BF)
        y = _dot(act_ref[...], wout_ref[...])
        o_ref[0, pl.ds(r0, FFN_TSUB), :] = x + half_gate * y
        return carry

    lax.fori_loop(0, FFN_TM // FFN_TSUB, sub, 0)


def _ffn(h, mod, gain, w_in, w_out, *, layer, mod0):
    B, S, D = h.shape
    kern = functools.partial(_ffn_kernel, mod0=mod0)
    return pl.pallas_call(
        kern,
        out_shape=jax.ShapeDtypeStruct((B, S, D), _F32),
        grid=(B, S // FFN_TM),
        in_specs=[
            pl.BlockSpec((1, FFN_TM, D), lambda b, i: (b, i, 0)),
            pl.BlockSpec((1, 1, N_MOD, D), lambda b, i: (layer, b, 0, 0)),
            pl.BlockSpec((None, 1, D), lambda b, i: (layer, 0, 0)),
            _resident((None, D, 2 * D_FF), lambda b, i: (layer, 0, 0)),
            _resident((None, D_FF, D), lambda b, i: (layer, 0, 0)),
        ],
        out_specs=pl.BlockSpec((1, FFN_TM, D), lambda b, i: (b, i, 0)),
        scratch_shapes=[pltpu.VMEM((FFN_TSUB, D_FF), _BF)],
        compiler_params=_params(),
        name="ffn",
    )(h, mod, gain, w_in, w_out)


def _norm_rope_heads(x, bd, n, gains, cos, sin, add=None):
    cos_g = cos * gains[0:1]
    sin_g = sin * gains[1:2]
    tiles = []
    for p0 in range(0, x.shape[1], 2 * LANES):
        xs = x[:, p0:p0 + 2 * LANES]
        if add is not None:
            xs = xs + add
        ss = _dot((xs * xs).astype(_BF), bd)
        xn = xs * lax.rsqrt(ss + n * NORM_EPS)
        for j in range(2):
            xh = xn[:, j * LANES:(j + 1) * LANES]
            tiles.append(xh * cos_g + pltpu.roll(xh, HALF, 1) * sin_g)
    return tiles


def _mix_in_kernel(h_ref, mod_ref, g_ref, win_ref, qan_ref, wuq_ref, kvan_ref, wuk_ref, wuv_ref,
                   qn_ref, kn_ref, gqn_ref, gkn_ref, rm_ref, rg_ref, bdm_ref, bdg_ref,
                   q_ref, k_ref, v_ref, gate_ref):
    tm = h_ref.shape[1]
    shift = mod_ref[0, 0, 3:4, :]
    scale1 = (1.0 + mod_ref[0, 0, 4:5, :]) * g_ref[...]
    u = (_rms(h_ref[0], 1.0 / D_MODEL) * scale1 + shift).astype(_BF)

    def proj(c0, c1):
        return _dot(u, win_ref[:, c0:c1])

    bd_m, bd_g = bdm_ref[...], bdg_ref[...]
    cos_m, sin_m = rm_ref[0], rm_ref[1]
    cos_g, sin_g = rg_ref[0], rg_ref[1]

    zq = proj(_C_Q, _C_KV)
    q_lat = (_rms(zq[:, :MLA_Q_RANK], 1.0 / MLA_Q_RANK) * qan_ref[...]).astype(_BF)
    k_pe = zq[:, MLA_Q_RANK:]
    for t, tile in enumerate(_norm_rope_heads(_dot(q_lat, wuq_ref[...]), bd_m, MLA_QK_DIM, qn_ref[...],
                                              cos_m, sin_m)):
        q_ref[0, t] = tile.astype(_BF)

    zkv = proj(_C_KV, _C_GQ)
    kv_lat = (_rms(zkv, 1.0 / MLA_KV_RANK) * kvan_ref[...]).astype(_BF)
    for t, tile in enumerate(_norm_rope_heads(_dot(kv_lat, wuk_ref[...]), bd_m, MLA_QK_DIM, kn_ref[...],
                                              cos_m, sin_m, add=jnp.concatenate([k_pe, k_pe], axis=1))):
        k_ref[0, t] = tile.astype(_BF)
    v_t = jnp.concatenate([_dot(kv_lat, wuv_ref[...]), proj(_C_GV, _C_END)], axis=1).T
    ones_rows = jnp.where(lax.broadcasted_iota(jnp.int32, (BF16_ROWS, tm), 0) == 0, 1.0, 0.0).astype(_BF)
    for kv in range(N_KV):
        v_ref[0, kv * V_ROWS:kv * V_ROWS + HEAD_V, :] = v_t[kv * HEAD_V:(kv + 1) * HEAD_V, :].astype(_BF)
        v_ref[0, kv * V_ROWS + HEAD_V:(kv + 1) * V_ROWS, :] = ones_rows

    for t, tile in enumerate(_norm_rope_heads(proj(_C_GQ, _C_GK), bd_g, GQA_HEAD_DIM, gqn_ref[...],
                                              cos_g, sin_g)):
        q_ref[0, MLA_HEADS + t] = tile.astype(_BF)
    lane_parity = lax.bitwise_and(lax.shift_right_logical(lax.broadcasted_iota(jnp.int32, (1, LANES), 1), 5), 1)
    for kv, tile in enumerate(_norm_rope_heads(proj(_C_GK, _C_GATE), bd_g, GQA_HEAD_DIM, gkn_ref[...],
                                               cos_g, sin_g)):
        for par in range(2):
            k_ref[0, MLA_HEADS + 2 * kv + par] = jnp.where(lane_parity == par, tile, 0.0).astype(_BF)

    zg = proj(_C_GATE, _C_GV)
    gate_ref[0] = (1.0 / (1.0 + jnp.exp(-zg))).astype(_BF)


def _mix_in(h, mod, gain, w, rope_m, rope_g, bd_m, bd_g, *, layer):
    B, S, D = h.shape
    tm = MIX_TM
    full = lambda shape: _resident(shape, lambda b, i: (0,) * len(shape))
    per_layer = lambda a: _resident((None,) + a.shape[1:], lambda b, i: (layer, 0, 0))
    out_shape = (
        jax.ShapeDtypeStruct((B, N_QT, S, LANES), _BF),
        jax.ShapeDtypeStruct((B, N_KT, S, LANES), _BF),
        jax.ShapeDtypeStruct((B, N_KV * V_ROWS, S), _BF),
        jax.ShapeDtypeStruct((B, S, 2 * D), _BF),
    )
    out_specs = (
        pl.BlockSpec((1, N_QT, tm, LANES), lambda b, i: (b, 0, i, 0)),
        pl.BlockSpec((1, N_KT, tm, LANES), lambda b, i: (b, 0, i, 0)),
        pl.BlockSpec((1, N_KV * V_ROWS, tm), lambda b, i: (b, 0, i)),
        pl.BlockSpec((1, tm, 2 * D), lambda b, i: (b, i, 0)),
    )
    names = ("w_in", "q_a_norm", "w_uq", "kv_a_norm", "w_uk", "w_uv",
             "qk_q_norm", "qk_k_norm", "gqa_q_norm", "gqa_k_norm")
    in_specs = [
        pl.BlockSpec((1, tm, D), lambda b, i: (b, i, 0)),
        pl.BlockSpec((1, 1, N_MOD, D), lambda b, i: (layer, b, 0, 0)),
        pl.BlockSpec((None, 1, D), lambda b, i: (layer, 0, 0)),
        *[per_layer(w[n]) for n in names],
        pl.BlockSpec((2, tm, LANES), lambda b, i: (0, i, 0)),
        pl.BlockSpec((2, tm, LANES), lambda b, i: (0, i, 0)),
        full(bd_m.shape),
        full(bd_g.shape),
    ]
    return pl.pallas_call(
        _mix_in_kernel,
        out_shape=out_shape,
        grid=(B, S // tm),
        in_specs=in_specs,
        out_specs=out_specs,
        compiler_params=_params(),
        name="mix_in",
    )(h, mod, gain, *[w[n] for n in names], rope_m, rope_g, bd_m, bd_g)


def _head_tiles(head):
    if isinstance(head, int):
        if head < MLA_HEADS:
            return head, head, head
        g = head - MLA_HEADS
        return MLA_HEADS + g // 2, MLA_HEADS + 2 * (g // GQA_GROUP) + g % 2, MLA_HEADS + g // GQA_GROUP
    g = head - MLA_HEADS
    kv = lax.shift_right_logical(g, 2)
    is_mla = head < MLA_HEADS
    return (jnp.where(is_mla, head, MLA_HEADS + lax.shift_right_logical(g, 1)),
            jnp.where(is_mla, head, MLA_HEADS + 2 * kv + lax.bitwise_and(g, 1)),
            jnp.where(is_mla, head, MLA_HEADS + kv))


def _attn_kernel(q_ref, k_ref, v_ref, o_ref, *scratch):
    s_refs, (m_ref, ot_ref) = scratch[:-2], scratch[-2:]
    n_keys = k_ref.shape[2]
    tq = s_refs[0].shape[1]
    n_units = (q_ref.shape[2] // tq) * N_Q
    chunk = min(KEY_CHUNK, n_keys)
    chunks = [(c0, chunk) for c0 in range(0, n_keys, chunk)]

    def unit(n):
        if isinstance(n, int):
            return n % N_Q, (n // N_Q) * tq
        head = lax.bitwise_and(n, N_Q - 1)
        return head, pl.multiple_of(lax.shift_right_logical(n, N_Q.bit_length() - 1) * tq, tq)

    def scores(n, slot):
        head, q0 = unit(n)
        q_tile, k_tile, _ = _head_tiles(head)
        q = q_ref[0, q_tile, pl.ds(q0, tq), :]
        acc = None
        for c0, cn in chunks:
            s_c = _dot_nt(k_ref[0, k_tile, c0:c0 + cn, :], q)
            s_refs[slot][c0:c0 + cn, :] = s_c
            for r0 in range(0, cn, MAX_ROWS):
                blk = s_c[r0:r0 + MAX_ROWS]
                acc = blk if acc is None else jnp.maximum(acc, blk)
            yield
        m_ref[slot] = jnp.max(acc, axis=0, keepdims=True)
        yield

    def combine(n, slot):
        head, _ = unit(n)
        r0 = _head_tiles(head)[2] * V_ROWS
        c_out = head * HEAD_V
        if not isinstance(n, int):
            r0 = pl.multiple_of(r0, BF16_ROWS)
            c_out = pl.multiple_of(c_out, HEAD_V)
        m = m_ref[slot]
        o_aug = None
        for c0, cn in chunks:
            p = jnp.exp2(s_refs[slot][c0:c0 + cn, :] - m).astype(_BF)
            part = _dot(v_ref[0, pl.ds(r0, V_ROWS), c0:c0 + cn], p)
            o_aug = part if o_aug is None else o_aug + part
            yield
        ot_ref[pl.ds(c_out, HEAD_V), :] = o_aug[:HEAD_V] / o_aug[HEAD_V:HEAD_V + 1]
        yield

    def run(*stages):
        live = list(stages)
        while live:
            for g in list(live):
                if next(g, _DONE) is _DONE:
                    live.remove(g)

    def flush(q0):
        o_ref[0, pl.ds(q0, tq), :] = ot_ref[...].T.astype(_BF)

    wide = len(s_refs) // 2
    n_groups = n_units // wide

    def score_group(first, bank):
        return [scores(first + j, bank * wide + j) for j in range(wide)]

    def combine_group(first, bank):
        return [combine(first + j, bank * wide + j) for j in range(wide)]

    run(*score_group(0, 0))

    def two_groups(i, carry):
        n = 2 * wide * i
        run(*score_group(n + wide, 1), *combine_group(n, 0))
        run(*score_group(n + 2 * wide, 0), *combine_group(n + wide, 1))
        head, q0 = unit(n + 2 * wide - 1)

        @pl.when(head == N_Q - 1)
        def _():
            flush(q0)

        return carry

    lax.fori_loop(0, n_groups // 2 - 1, two_groups, 0)
    last = n_units - wide
    run(*score_group(last, 1), *combine_group(last - wide, 0))
    run(*combine_group(last, 1))
    flush(unit(n_units - 1)[1])


def _attn(q, k, v_t):
    B, _, S, _ = q.shape
    tq = ATT_TQ
    return pl.pallas_call(
        _attn_kernel,
        out_shape=jax.ShapeDtypeStruct((B, S, N_Q * HEAD_V), _BF),
        grid=(B,),
        in_specs=[
            pl.BlockSpec((1, N_QT, S, LANES), lambda b: (b, 0, 0, 0)),
            pl.BlockSpec((1, N_KT, S, LANES), lambda b: (b, 0, 0, 0)),
            pl.BlockSpec((1, N_KV * V_ROWS, S), lambda b: (b, 0, 0)),
        ],
        out_specs=pl.BlockSpec((1, S, N_Q * HEAD_V), lambda b: (b, 0, 0)),
        scratch_shapes=[
            *[pltpu.VMEM((S, tq), _F32) for _ in range(2 * ATT_WIDE)],
            pltpu.VMEM((2 * ATT_WIDE, 1, tq), _F32),
            pltpu.VMEM((N_Q * HEAD_V, tq), _F32),
        ],
        compiler_params=pltpu.CompilerParams(dimension_semantics=("arbitrary",),
                                             vmem_limit_bytes=VMEM_LIMIT),
        name="attn",
    )(q, k, v_t)


def _mix_out_kernel(h_ref, mod_ref, o_ref_in, gate_ref, wbm_ref, wbg_ref, wo_ref, o_ref):
    g2 = mod_ref[0, 0, 5:6, :]
    n_mla = MLA_HEADS * HEAD_V
    bm = _dot(o_ref_in[0, :, :n_mla], wbm_ref[...])
    bg = _dot(o_ref_in[0, :, n_mla:], wbg_ref[...])
    merged = gate_ref[0, :, :D_MODEL] * bm + gate_ref[0, :, D_MODEL:] * bg
    o_ref[0] = h_ref[0] + g2 * _dot(merged.astype(_BF), wo_ref[...])


def _mix_out(h, mod, o_att, gate, wb_mla, wb_gqa, w_out, *, layer):
    B, S, D = h.shape
    tm = OUT_TM
    row = lambda n: pl.BlockSpec((1, tm, n), lambda b, i: (b, i, 0))
    wspec = lambda a: _resident((None,) + a.shape[1:], lambda b, i: (layer, 0, 0))
    return pl.pallas_call(
        _mix_out_kernel,
        out_shape=jax.ShapeDtypeStruct((B, S, D), _F32),
        grid=(B, S // tm),
        in_specs=[
            row(D),
            pl.BlockSpec((1, 1, N_MOD, D), lambda b, i: (layer, b, 0, 0)),
            row(o_att.shape[-1]), row(2 * D),
            wspec(wb_mla), wspec(wb_gqa), wspec(w_out),
        ],
        out_specs=row(D),
        compiler_params=_params(),
        name="mix_out",
    )(h, mod, o_att, gate, wb_mla, wb_gqa, w_out)


def _rope_angles(pos, dim):
    inv = ROPE_THETA ** (-jnp.arange(0, dim, 2, dtype=_F32) / dim)
    ang = pos.astype(_F32)[:, None] * inv[None, :]
    return jnp.cos(ang), jnp.sin(ang)


def _rope_tables(S):
    t = jnp.arange(S)
    one = lambda n: jnp.ones((S, n), _F32)
    zero = lambda n: jnp.zeros((S, n), _F32)
    c, s = _rope_angles(t, MLA_ROPE_DIM)
    rope_m = jnp.stack([
        jnp.concatenate([c, one(HALF - ROT), c, one(HALF - ROT)], axis=1),
        jnp.concatenate([-s, zero(HALF - ROT), s, zero(HALF - ROT)], axis=1),
    ])
    cr, sr = _rope_angles(t // GRID_W, GQA_HEAD_DIM // 2)
    cc, sc = _rope_angles(t % GRID_W, GQA_HEAD_DIM // 2)
    sign = jnp.where(jnp.arange(LANES) < HALF, -1.0, 1.0).astype(_F32)
    rope_g = jnp.stack([
        jnp.tile(jnp.concatenate([cr, cc], axis=1), (1, LANES // (2 * ROT))),
        jnp.tile(jnp.concatenate([sr, sc], axis=1), (1, LANES // (2 * ROT))) * sign[None, :],
    ])
    return rope_m, rope_g


def _to_lanes(w, heads, dim, lane_dim):
    lead = w.shape[:-1]
    w = w.reshape(lead + (heads, dim))
    w = jnp.concatenate([w, jnp.zeros(lead + (heads, 1), w.dtype)], axis=-1)
    w = jnp.take(w, jnp.asarray(np.where(lane_dim < 0, dim, lane_dim)), axis=-1)
    return w.reshape(lead + (heads * LANES,))


def _take_cols(w, cols):
    return jnp.take(w, jnp.asarray(cols), axis=-1)


_GQ_COLS = np.concatenate([(2 * t + _GQA_LANE_PARITY) * GQA_HEAD_DIM + _GQA_LANE_DIM
                           for t in range(GQA_Q_HEADS // 2)])
_GK_COLS = np.concatenate([kv * GQA_HEAD_DIM + _GQA_LANE_DIM for kv in range(GQA_KV_HEADS)])


def _gain_rows(g):
    return jnp.stack([g, jnp.roll(g, HALF, axis=-1)], axis=-2)


def _pack_mix_weights(w_in, q_a_norm, w_uq, kv_a_norm, w_ukv, qk_q_norm, qk_k_norm, gqa_q_norm, gqa_k_norm):
    L, D = w_in.shape[0], D_MODEL
    sizes = (MLA_Q_RANK, MLA_KV_RANK, MLA_ROPE_DIM, GQA_Q_HEADS * GQA_HEAD_DIM,
             GQA_KV_HEADS * GQA_HEAD_DIM, GQA_KV_HEADS * GQA_HEAD_DIM, 2 * D)
    parts, start = [], 0
    for s in sizes:
        parts.append(w_in[..., start:start + s])
        start += s
    zq, zkv, zkr, gq, gk, gv, gate = parts
    kr_tile = _to_lanes(jnp.pad(zkr, ((0, 0), (0, 0), (MLA_NOPE_DIM, 0))), 1, MLA_QK_DIM, _MLA_LANE_DIM)
    w_in_p = jnp.concatenate([zq, kr_tile, zkv, _take_cols(gq, _GQ_COLS), _take_cols(gk, _GK_COLS), gate, gv],
                             axis=-1)
    ukv = w_ukv.reshape(L, MLA_KV_RANK, MLA_HEADS, MLA_NOPE_DIM + MLA_V_DIM)
    uk = jnp.pad(ukv[..., :MLA_NOPE_DIM], ((0, 0), (0, 0), (0, 0), (0, MLA_ROPE_DIM))).reshape(L, MLA_KV_RANK, -1)
    w_uv = ukv[..., MLA_NOPE_DIM:].reshape(L, MLA_KV_RANK, -1)
    return {
        "w_in": w_in_p.astype(_BF),
        "q_a_norm": q_a_norm[:, None, :],
        "w_uq": _to_lanes(w_uq, MLA_HEADS, MLA_QK_DIM, _MLA_LANE_DIM).astype(_BF),
        "kv_a_norm": kv_a_norm[:, None, :],
        "w_uk": _to_lanes(uk, MLA_HEADS, MLA_QK_DIM, _MLA_LANE_DIM).astype(_BF),
        "w_uv": w_uv.astype(_BF),
        "qk_q_norm": _gain_rows(_to_lanes(qk_q_norm * LOG2E, 1, MLA_QK_DIM, _MLA_LANE_DIM)),
        "qk_k_norm": _gain_rows(_to_lanes(qk_k_norm * MLA_QK_DIM ** 0.5, 1, MLA_QK_DIM, _MLA_LANE_DIM)),
        "gqa_q_norm": _gain_rows(_take_cols(gqa_q_norm * LOG2E, _GQA_LANE_DIM)),
        "gqa_k_norm": _gain_rows(_take_cols(gqa_k_norm * GQA_HEAD_DIM ** 0.5, _GQA_LANE_DIM)),
    }


def kernel(x, c, w_ada, b_ada, norm_ffn1, w_ffn1_in, w_ffn1_out, norm_mix, w_in, mla_q_a_norm, mla_w_uq,
           mla_kv_a_norm, mla_w_ukv, mla_qk_q_norm, mla_qk_k_norm, gqa_q_norm, gqa_k_norm, w_branch_mla,
           w_branch_gqa, w_out, norm_ffn2, w_ffn2_in, w_ffn2_out):
    B, S, D = x.shape
    mod = _ada(c, w_ada, b_ada).reshape(DEPTH, B, N_MOD, D)
    rope_m, rope_g = _rope_tables(S)
    lane = np.arange(2 * LANES)
    same_tile = (lane[:, None] // LANES) == (lane[None, :] // LANES)
    same_parity = _GQA_LANE_PARITY[lane % LANES][:, None] == _GQA_LANE_PARITY[lane % LANES][None, :]
    bd_m = jnp.asarray(same_tile, _BF)
    bd_g = jnp.asarray(same_tile & same_parity, _BF)
    w1_in, w1_out = w_ffn1_in.astype(_BF), w_ffn1_out.astype(_BF)
    w2_in, w2_out = w_ffn2_in.astype(_BF), w_ffn2_out.astype(_BF)
    wb_mla, wb_gqa, wo = w_branch_mla.astype(_BF), w_branch_gqa.astype(_BF), w_out.astype(_BF)
    norm_ffn1, norm_mix, norm_ffn2 = (g.reshape(DEPTH, 1, D) for g in (norm_ffn1, norm_mix, norm_ffn2))

    wm = _pack_mix_weights(w_in, mla_q_a_norm, mla_w_uq, mla_kv_a_norm, mla_w_ukv,
                           mla_qk_q_norm, mla_qk_k_norm, gqa_q_norm, gqa_k_norm)

    h = x
    for l in range(DEPTH):
        h = _ffn(h, mod, norm_ffn1, w1_in, w1_out, layer=l, mod0=0)
        q, k, v_t, gate = _mix_in(h, mod, norm_mix, wm, rope_m, rope_g, bd_m, bd_g, layer=l)
        o_att = _attn(q, k, v_t)
        h = _mix_out(h, mod, o_att, gate, wb_mla, wb_gqa, wo, layer=l)
        h = _ffn(h, mod, norm_ffn2, w2_in, w2_out, layer=l, mod0=6)
    return h
```

```python
import functools
import math

import jax
import jax.numpy as jnp
import numpy as np
from jax import lax
from jax.experimental import pallas as pl
from jax.experimental.pallas import tpu as pltpu

D_MODEL = 1024
DEPTH = 2
GRID_W = 64
ROPE_THETA = 10000.0
NORM_EPS = 1e-6
MLA_HEADS = 8
MLA_Q_RANK = 384
MLA_KV_RANK = 256
MLA_NOPE_DIM = 64
MLA_ROPE_DIM = 32
MLA_V_DIM = 64
MLA_QK_DIM = MLA_NOPE_DIM + MLA_ROPE_DIM
GQA_Q_HEADS = 8
GQA_KV_HEADS = 2
GQA_GROUP = GQA_Q_HEADS // GQA_KV_HEADS
GQA_HEAD_DIM = 64
D_FF = 2816
N_MOD = 9

LANES = 128
HALF = LANES // 2
ROT = 16
BF16_ROWS = 16
LOG2E = math.log2(math.e)

HEAD_V = 64
V_ROWS = HEAD_V + BF16_ROWS
N_Q = MLA_HEADS + GQA_Q_HEADS
N_KV = MLA_HEADS + GQA_KV_HEADS
N_QT = MLA_HEADS + GQA_Q_HEADS // 2
N_KT = MLA_HEADS + 2 * GQA_KV_HEADS

_MLA_LANE_DIM = np.full(LANES, -1, np.int64)
_MLA_LANE_DIM[0:ROT] = MLA_NOPE_DIM + np.arange(ROT)
_MLA_LANE_DIM[HALF:HALF + ROT] = MLA_NOPE_DIM + ROT + np.arange(ROT)
_MLA_LANE_DIM[ROT:HALF] = np.arange(HALF - ROT)
_MLA_LANE_DIM[HALF + ROT:HALF + 2 * ROT] = HALF - ROT + np.arange(ROT)
_GQA_LANE_PARITY = (np.arange(LANES) // (2 * ROT)) % 2
_GQA_LANE_DIM = (2 * ROT * ((np.arange(LANES) // ROT) % 2)
                 + ROT * (np.arange(LANES) // HALF)
                 + np.arange(LANES) % ROT)

_C_Q = 0
_C_KR = _C_Q + MLA_Q_RANK
_C_KV = _C_KR + LANES
_C_GQ = _C_KV + MLA_KV_RANK
_C_GK = _C_GQ + (GQA_Q_HEADS // 2) * LANES
_C_GATE = _C_GK + GQA_KV_HEADS * LANES
_C_GV = _C_GATE + 2 * D_MODEL
_C_END = _C_GV + GQA_KV_HEADS * GQA_HEAD_DIM

VMEM_LIMIT = 56 * 1024 * 1024

FFN_TM = 1024
FFN_TSUB = 1024
FFN_TF = 256
MIX_TM = 1024
ATT_TQ = 512
ATT_WIDE = 2
ATT_QSPLIT = 2
MAX_ROWS = 32
KEY_CHUNK = 2048
_DONE = object()
OUT_TM = 1024

_BF = jnp.bfloat16
_F32 = jnp.float32


def _params():
    return pltpu.CompilerParams(dimension_semantics=("arbitrary", "arbitrary"),
                                vmem_limit_bytes=VMEM_LIMIT)


def _dot(a, b):
    return jnp.dot(a, b, preferred_element_type=_F32)


def _dot_nt(a, b):
    return lax.dot_general(a, b, (((1,), (1,)), ((), ())), preferred_element_type=_F32)


def _resident(block_shape, index_map):
    return pl.BlockSpec(block_shape, index_map, pipeline_mode=pl.Buffered(1))


def _rms(x, inv_n):
    return x * lax.rsqrt(jnp.sum(x * x, axis=-1, keepdims=True) * inv_n + NORM_EPS)


def _ada_kernel(c_ref, w_ref, b_ref, o_ref):
    c = c_ref[...]
    c_act = c / (1.0 + jnp.exp(-c))
    o_ref[0] = _dot(c_act.astype(_BF), w_ref[0].astype(_BF)) + b_ref[0]


def _ada(c, w_ada, b_ada):
    L, D, N = w_ada.shape
    B = c.shape[0]
    tn = D_MODEL
    return pl.pallas_call(
        _ada_kernel,
        out_shape=jax.ShapeDtypeStruct((L, B, N), _F32),
        grid=(L, N // tn),
        in_specs=[
            pl.BlockSpec((B, D), lambda l, j: (0, 0)),
            pl.BlockSpec((1, D, tn), lambda l, j: (l, 0, j)),
            pl.BlockSpec((1, 1, tn), lambda l, j: (l, 0, j)),
        ],
        out_specs=pl.BlockSpec((1, B, tn), lambda l, j: (l, 0, j)),
        compiler_params=_params(),
        name="ada",
    )(c, w_ada, b_ada.reshape(L, 1, N))


def _ffn_kernel(h_ref, mod_ref, g_ref, win_ref, wout_ref, o_ref, act_ref, *, mod0):
    shift = mod_ref[0, 0, mod0:mod0 + 1, :]
    scale1 = (1.0 + mod_ref[0, 0, mod0 + 1:mod0 + 2, :]) * g_ref[...]
    half_gate = 0.5 * mod_ref[0, 0, mod0 + 2:mod0 + 3, :]

    def sub(r, carry):
        r0 = pl.multiple_of(r * FFN_TSUB, FFN_TSUB)
        x = h_ref[0, pl.ds(r0, FFN_TSUB), :]
        u = (_rms(x, 1.0 / D_MODEL) * scale1 + shift).astype(_BF)
        for c0 in range(0, D_FF, FFN_TF):
            a = _dot(u, win_ref[:, c0:c0 + FFN_TF])
            b = _dot(u, win_ref[:, D_FF + c0:D_FF + c0 + FFN_TF])
            act_ref[:, c0:c0 + FFN_TF] = (a / (1.0 + jnp.exp(-a)) * b).astype(_BF)
        y = _dot(act_ref[...], wout_ref[...])
        o_ref[0, pl.ds(r0, FFN_TSUB), :] = x + half_gate * y
        return carry

    lax.fori_loop(0, FFN_TM // FFN_TSUB, sub, 0)


def _ffn(h, mod, gain, w_in, w_out, *, layer, mod0):
    B, S, D = h.shape
    kern = functools.partial(_ffn_kernel, mod0=mod0)
    return pl.pallas_call(
        kern,
        out_shape=jax.ShapeDtypeStruct((B, S, D), _F32),
        grid=(B, S // FFN_TM),
        in_specs=[
            pl.BlockSpec((1, FFN_TM, D), lambda b, i: (b, i, 0)),
            pl.BlockSpec((1, 1, N_MOD, D), lambda b, i: (layer, b, 0, 0)),
            pl.BlockSpec((None, 1, D), lambda b, i: (layer, 0, 0)),
            _resident((None, D, 2 * D_FF), lambda b, i: (layer, 0, 0)),
            _resident((None, D_FF, D), lambda b, i: (layer, 0, 0)),
        ],
        out_specs=pl.BlockSpec((1, FFN_TM, D), lambda b, i: (b, i, 0)),
        scratch_shapes=[pltpu.VMEM((FFN_TSUB, D_FF), _BF)],
        compiler_params=_params(),
        name="ffn",
    )(h, mod, gain, w_in, w_out)


def _norm_rope_heads(x, bd, n, gains, cos, sin, add=None):
    cos_g = cos * gains[0:1]
    sin_g = sin * gains[1:2]
    tiles = []
    for p0 in range(0, x.shape[1], 2 * LANES):
        xs = x[:, p0:p0 + 2 * LANES]
        if add is not None:
            xs = xs + add
        ss = _dot((xs * xs).astype(_BF), bd)
        xn = xs * lax.rsqrt(ss + n * NORM_EPS)
        for j in range(2):
            xh = xn[:, j * LANES:(j + 1) * LANES]
            tiles.append(xh * cos_g + pltpu.roll(xh, HALF, 1) * sin_g)
    return tiles


def _mix_in_kernel(h_ref, mod_ref, g_ref, win_ref, qan_ref, wuq_ref, kvan_ref, wuk_ref, wuv_ref,
                   qn_ref, kn_ref, gqn_ref, gkn_ref, rm_ref, rg_ref, bdm_ref, bdg_ref,
                   q_ref, k_ref, v_ref, gate_ref):
    tm = h_ref.shape[1]
    shift = mod_ref[0, 0, 3:4, :]
    scale1 = (1.0 + mod_ref[0, 0, 4:5, :]) * g_ref[...]
    u = (_rms(h_ref[0], 1.0 / D_MODEL) * scale1 + shift).astype(_BF)

    def proj(c0, c1):
        return _dot(u, win_ref[:, c0:c1])

    bd_m, bd_g = bdm_ref[...], bdg_ref[...]
    cos_m, sin_m = rm_ref[0], rm_ref[1]
    cos_g, sin_g = rg_ref[0], rg_ref[1]

    zq = proj(_C_Q, _C_KV)
    q_lat = (_rms(zq[:, :MLA_Q_RANK], 1.0 / MLA_Q_RANK) * qan_ref[...]).astype(_BF)
    k_pe = zq[:, MLA_Q_RANK:]
    for t, tile in enumerate(_norm_rope_heads(_dot(q_lat, wuq_ref[...]), bd_m, MLA_QK_DIM, qn_ref[...],
                                              cos_m, sin_m)):
        q_ref[0, t] = tile.astype(_BF)

    zkv = proj(_C_KV, _C_GQ)
    kv_lat = (_rms(zkv, 1.0 / MLA_KV_RANK) * kvan_ref[...]).astype(_BF)
    for t, tile in enumerate(_norm_rope_heads(_dot(kv_lat, wuk_ref[...]), bd_m, MLA_QK_DIM, kn_ref[...],
                                              cos_m, sin_m, add=jnp.concatenate([k_pe, k_pe], axis=1))):
        k_ref[0, t] = tile.astype(_BF)
    v_t = jnp.concatenate([_dot(kv_lat, wuv_ref[...]), proj(_C_GV, _C_END)], axis=1).T
    ones_rows = jnp.where(lax.broadcasted_iota(jnp.int32, (BF16_ROWS, tm), 0) == 0, 1.0, 0.0).astype(_BF)
    for kv in range(N_KV):
        v_ref[0, kv * V_ROWS:kv * V_ROWS + HEAD_V, :] = v_t[kv * HEAD_V:(kv + 1) * HEAD_V, :].astype(_BF)
        v_ref[0, kv * V_ROWS + HEAD_V:(kv + 1) * V_ROWS, :] = ones_rows

    for t, tile in enumerate(_norm_rope_heads(proj(_C_GQ, _C_GK), bd_g, GQA_HEAD_DIM, gqn_ref[...],
                                              cos_g, sin_g)):
        q_ref[0, MLA_HEADS + t] = tile.astype(_BF)
    lane_parity = lax.bitwise_and(lax.shift_right_logical(lax.broadcasted_iota(jnp.int32, (1, LANES), 1), 5), 1)
    for kv, tile in enumerate(_norm_rope_heads(proj(_C_GK, _C_GATE), bd_g, GQA_HEAD_DIM, gkn_ref[...],
                                               cos_g, sin_g)):
        for par in range(2):
            k_ref[0, MLA_HEADS + 2 * kv + par] = jnp.where(lane_parity == par, tile, 0.0).astype(_BF)

    zg = proj(_C_GATE, _C_GV)
    gate_ref[0] = (1.0 / (1.0 + jnp.exp(-zg))).astype(_BF)


def _mix_in(h, mod, gain, w, rope_m, rope_g, bd_m, bd_g, *, layer):
    B, S, D = h.shape
    tm = MIX_TM
    full = lambda shape: _resident(shape, lambda b, i: (0,) * len(shape))
    per_layer = lambda a: _resident((None,) + a.shape[1:], lambda b, i: (layer, 0, 0))
    out_shape = (
        jax.ShapeDtypeStruct((B, N_QT, S, LANES), _BF),
        jax.ShapeDtypeStruct((B, N_KT, S, LANES), _BF),
        jax.ShapeDtypeStruct((B, N_KV * V_ROWS, S), _BF),
        jax.ShapeDtypeStruct((B, S, 2 * D), _BF),
    )
    out_specs = (
        pl.BlockSpec((1, N_QT, tm, LANES), lambda b, i: (b, 0, i, 0)),
        pl.BlockSpec((1, N_KT, tm, LANES), lambda b, i: (b, 0, i, 0)),
        pl.BlockSpec((1, N_KV * V_ROWS, tm), lambda b, i: (b, 0, i)),
        pl.BlockSpec((1, tm, 2 * D), lambda b, i: (b, i, 0)),
    )
    names = ("w_in", "q_a_norm", "w_uq", "kv_a_norm", "w_uk", "w_uv",
             "qk_q_norm", "qk_k_norm", "gqa_q_norm", "gqa_k_norm")
    in_specs = [
        pl.BlockSpec((1, tm, D), lambda b, i: (b, i, 0)),
        pl.BlockSpec((1, 1, N_MOD, D), lambda b, i: (layer, b, 0, 0)),
        pl.BlockSpec((None, 1, D), lambda b, i: (layer, 0, 0)),
        *[per_layer(w[n]) for n in names],
        pl.BlockSpec((2, tm, LANES), lambda b, i: (0, i, 0)),
        pl.BlockSpec((2, tm, LANES), lambda b, i: (0, i, 0)),
        full(bd_m.shape),
        full(bd_g.shape),
    ]
    return pl.pallas_call(
        _mix_in_kernel,
        out_shape=out_shape,
        grid=(B, S // tm),
        in_specs=in_specs,
        out_specs=out_specs,
        compiler_params=_params(),
        name="mix_in",
    )(h, mod, gain, *[w[n] for n in names], rope_m, rope_g, bd_m, bd_g)


def _head_tiles(head):
    if isinstance(head, int):
        if head < MLA_HEADS:
            return head, head, head
        g = head - MLA_HEADS
        return MLA_HEADS + g // 2, MLA_HEADS + 2 * (g // GQA_GROUP) + g % 2, MLA_HEADS + g // GQA_GROUP
    g = head - MLA_HEADS
    kv = lax.shift_right_logical(g, 2)
    is_mla = head < MLA_HEADS
    return (jnp.where(is_mla, head, MLA_HEADS + lax.shift_right_logical(g, 1)),
            jnp.where(is_mla, head, MLA_HEADS + 2 * kv + lax.bitwise_and(g, 1)),
            jnp.where(is_mla, head, MLA_HEADS + kv))


def _attn_kernel(q_ref, k_ref, v_ref, o_ref, *scratch):
    s_refs, (m_ref, ot_ref) = scratch[:-2], scratch[-2:]
    n_keys = k_ref.shape[2]
    tq = s_refs[0].shape[1]
    n_units = (q_ref.shape[2] // tq) * N_Q
    chunk = min(KEY_CHUNK, n_keys)
    chunks = [(c0, chunk) for c0 in range(0, n_keys, chunk)]

    def unit(n):
        if isinstance(n, int):
            return n % N_Q, (n // N_Q) * tq
        head = lax.bitwise_and(n, N_Q - 1)
        return head, pl.multiple_of(lax.shift_right_logical(n, N_Q.bit_length() - 1) * tq, tq)

    def scores(n, slot):
        head, q0 = unit(n)
        q_tile, k_tile, _ = _head_tiles(head)
        q = q_ref[0, q_tile, pl.ds(q0, tq), :]
        acc = None
        for c0, cn in chunks:
            s_c = _dot_nt(k_ref[0, k_tile, c0:c0 + cn, :], q)
            s_refs[slot][c0:c0 + cn, :] = s_c
            for r0 in range(0, cn, MAX_ROWS):
                blk = s_c[r0:r0 + MAX_ROWS]
                acc = blk if acc is None else jnp.maximum(acc, blk)
            yield
        m_ref[slot] = jnp.max(acc, axis=0, keepdims=True)
        yield

    def combine(n, slot):
        head, _ = unit(n)
        r0 = _head_tiles(head)[2] * V_ROWS
        c_out = head * HEAD_V
        if not isinstance(n, int):
            r0 = pl.multiple_of(r0, BF16_ROWS)
            c_out = pl.multiple_of(c_out, HEAD_V)
        m = m_ref[slot]
        o_aug = None
        for c0, cn in chunks:
            p = jnp.exp2(s_refs[slot][c0:c0 + cn, :] - m).astype(_BF)
            part = _dot(v_ref[0, pl.ds(r0, V_ROWS), c0:c0 + cn], p)
            o_aug = part if o_aug is None else o_aug + part
            yield
        ot_ref[pl.ds(c_out, HEAD_V), :] = o_aug[:HEAD_V] / o_aug[HEAD_V:HEAD_V + 1]
        yield

    def run(*stages):
        live = list(stages)
        while live:
            for g in list(live):
                if next(g, _DONE) is _DONE:
                    live.remove(g)

    def flush(q0):
        o_ref[0, pl.ds(q0, tq), :] = ot_ref[...].T.astype(_BF)

    wide = len(s_refs) // 2
    n_groups = n_units // wide

    def score_group(first, bank):
        return [scores(first + j, bank * wide + j) for j in range(wide)]

    def combine_group(first, bank):
        return [combine(first + j, bank * wide + j) for j in range(wide)]

    run(*score_group(0, 0))

    def two_groups(i, carry):
        n = 2 * wide * i
        run(*score_group(n + wide, 1), *combine_group(n, 0))
        run(*score_group(n + 2 * wide, 0), *combine_group(n + wide, 1))
        head, q0 = unit(n + 2 * wide - 1)

        @pl.when(head == N_Q - 1)
        def _():
            flush(q0)

        return carry

    lax.fori_loop(0, n_groups // 2 - 1, two_groups, 0)
    last = n_units - wide
    run(*score_group(last, 1), *combine_group(last - wide, 0))
    run(*combine_group(last, 1))
    flush(unit(n_units - 1)[1])


def _attn(q, k, v_t):
    B, _, S, _ = q.shape
    tq = ATT_TQ
    return pl.pallas_call(
        _attn_kernel,
        out_shape=jax.ShapeDtypeStruct((B, S, N_Q * HEAD_V), _BF),
        grid=(B, ATT_QSPLIT),
        in_specs=[
            pl.BlockSpec((1, N_QT, S // ATT_QSPLIT, LANES), lambda b, i: (b, 0, i, 0)),
            pl.BlockSpec((1, N_KT, S, LANES), lambda b, i: (b, 0, 0, 0)),
            pl.BlockSpec((1, N_KV * V_ROWS, S), lambda b, i: (b, 0, 0)),
        ],
        out_specs=pl.BlockSpec((1, S // ATT_QSPLIT, N_Q * HEAD_V), lambda b, i: (b, i, 0)),
        scratch_shapes=[
            *[pltpu.VMEM((S, tq), _F32) for _ in range(2 * ATT_WIDE)],
            pltpu.VMEM((2 * ATT_WIDE, 1, tq), _F32),
            pltpu.VMEM((N_Q * HEAD_V, tq), _F32),
        ],
        compiler_params=_params(),
        name="attn",
    )(q, k, v_t)


def _mix_out_kernel(h_ref, mod_ref, o_ref_in, gate_ref, wbm_ref, wbg_ref, wo_ref, o_ref):
    g2 = mod_ref[0, 0, 5:6, :]
    n_mla = MLA_HEADS * HEAD_V
    bm = _dot(o_ref_in[0, :, :n_mla], wbm_ref[...])
    bg = _dot(o_ref_in[0, :, n_mla:], wbg_ref[...])
    merged = gate_ref[0, :, :D_MODEL] * bm + gate_ref[0, :, D_MODEL:] * bg
    o_ref[0] = h_ref[0] + g2 * _dot(merged.astype(_BF), wo_ref[...])


def _mix_out(h, mod, o_att, gate, wb_mla, wb_gqa, w_out, *, layer):
    B, S, D = h.shape
    tm = OUT_TM
    row = lambda n: pl.BlockSpec((1, tm, n), lambda b, i: (b, i, 0))
    wspec = lambda a: _resident((None,) + a.shape[1:], lambda b, i: (layer, 0, 0))
    return pl.pallas_call(
        _mix_out_kernel,
        out_shape=jax.ShapeDtypeStruct((B, S, D), _F32),
        grid=(B, S // tm),
        in_specs=[
            row(D),
            pl.BlockSpec((1, 1, N_MOD, D), lambda b, i: (layer, b, 0, 0)),
            row(o_att.shape[-1]), row(2 * D),
            wspec(wb_mla), wspec(wb_gqa), wspec(w_out),
        ],
        out_specs=row(D),
        compiler_params=_params(),
        name="mix_out",
    )(h, mod, o_att, gate, wb_mla, wb_gqa, w_out)


def _rope_angles(pos, dim):
    inv = ROPE_THETA ** (-jnp.arange(0, dim, 2, dtype=_F32) / dim)
    ang = pos.astype(_F32)[:, None] * inv[None, :]
    return jnp.cos(ang), jnp.sin(ang)


def _rope_tables(S):
    t = jnp.arange(S)
    one = lambda n: jnp.ones((S, n), _F32)
    zero = lambda n: jnp.zeros((S, n), _F32)
    c, s = _rope_angles(t, MLA_ROPE_DIM)
    rope_m = jnp.stack([
        jnp.concatenate([c, one(HALF - ROT), c, one(HALF - ROT)], axis=1),
        jnp.concatenate([-s, zero(HALF - ROT), s, zero(HALF - ROT)], axis=1),
    ])
    cr, sr = _rope_angles(t // GRID_W, GQA_HEAD_DIM // 2)
    cc, sc = _rope_angles(t % GRID_W, GQA_HEAD_DIM // 2)
    sign = jnp.where(jnp.arange(LANES) < HALF, -1.0, 1.0).astype(_F32)
    rope_g = jnp.stack([
        jnp.tile(jnp.concatenate([cr, cc], axis=1), (1, LANES // (2 * ROT))),
        jnp.tile(jnp.concatenate([sr, sc], axis=1), (1, LANES // (2 * ROT))) * sign[None, :],
    ])
    return rope_m, rope_g


def _to_lanes(w, heads, dim, lane_dim):
    lead = w.shape[:-1]
    w = w.reshape(lead + (heads, dim))
    w = jnp.concatenate([w, jnp.zeros(lead + (heads, 1), w.dtype)], axis=-1)
    w = jnp.take(w, jnp.asarray(np.where(lane_dim < 0, dim, lane_dim)), axis=-1)
    return w.reshape(lead + (heads * LANES,))


def _take_cols(w, cols):
    return jnp.take(w, jnp.asarray(cols), axis=-1)


_GQ_COLS = np.concatenate([(2 * t + _GQA_LANE_PARITY) * GQA_HEAD_DIM + _GQA_LANE_DIM
                           for t in range(GQA_Q_HEADS // 2)])
_GK_COLS = np.concatenate([kv * GQA_HEAD_DIM + _GQA_LANE_DIM for kv in range(GQA_KV_HEADS)])


def _gain_rows(g):
    return jnp.stack([g, jnp.roll(g, HALF, axis=-1)], axis=-2)


def _pack_mix_weights(w_in, q_a_norm, w_uq, kv_a_norm, w_ukv, qk_q_norm, qk_k_norm, gqa_q_norm, gqa_k_norm):
    L, D = w_in.shape[0], D_MODEL
    sizes = (MLA_Q_RANK, MLA_KV_RANK, MLA_ROPE_DIM, GQA_Q_HEADS * GQA_HEAD_DIM,
             GQA_KV_HEADS * GQA_HEAD_DIM, GQA_KV_HEADS * GQA_HEAD_DIM, 2 * D)
    parts, start = [], 0
    for s in sizes:
        parts.append(w_in[..., start:start + s])
        start += s
    zq, zkv, zkr, gq, gk, gv, gate = parts
    kr_tile = _to_lanes(jnp.pad(zkr, ((0, 0), (0, 0), (MLA_NOPE_DIM, 0))), 1, MLA_QK_DIM, _MLA_LANE_DIM)
    w_in_p = jnp.concatenate([zq, kr_tile, zkv, _take_cols(gq, _GQ_COLS), _take_cols(gk, _GK_COLS), gate, gv],
                             axis=-1)
    ukv = w_ukv.reshape(L, MLA_KV_RANK, MLA_HEADS, MLA_NOPE_DIM + MLA_V_DIM)
    uk = jnp.pad(ukv[..., :MLA_NOPE_DIM], ((0, 0), (0, 0), (0, 0), (0, MLA_ROPE_DIM))).reshape(L, MLA_KV_RANK, -1)
    w_uv = ukv[..., MLA_NOPE_DIM:].reshape(L, MLA_KV_RANK, -1)
    return {
        "w_in": w_in_p.astype(_BF),
        "q_a_norm": q_a_norm[:, None, :],
        "w_uq": _to_lanes(w_uq, MLA_HEADS, MLA_QK_DIM, _MLA_LANE_DIM).astype(_BF),
        "kv_a_norm": kv_a_norm[:, None, :],
        "w_uk": _to_lanes(uk, MLA_HEADS, MLA_QK_DIM, _MLA_LANE_DIM).astype(_BF),
        "w_uv": w_uv.astype(_BF),
        "qk_q_norm": _gain_rows(_to_lanes(qk_q_norm * LOG2E, 1, MLA_QK_DIM, _MLA_LANE_DIM)),
        "qk_k_norm": _gain_rows(_to_lanes(qk_k_norm * MLA_QK_DIM ** 0.5, 1, MLA_QK_DIM, _MLA_LANE_DIM)),
        "gqa_q_norm": _gain_rows(_take_cols(gqa_q_norm * LOG2E, _GQA_LANE_DIM)),
        "gqa_k_norm": _gain_rows(_take_cols(gqa_k_norm * GQA_HEAD_DIM ** 0.5, _GQA_LANE_DIM)),
    }


def kernel(x, c, w_ada, b_ada, norm_ffn1, w_ffn1_in, w_ffn1_out, norm_mix, w_in, mla_q_a_norm, mla_w_uq,
           mla_kv_a_norm, mla_w_ukv, mla_qk_q_norm, mla_qk_k_norm, gqa_q_norm, gqa_k_norm, w_branch_mla,
           w_branch_gqa, w_out, norm_ffn2, w_ffn2_in, w_ffn2_out):
    B, S, D = x.shape
    mod = _ada(c, w_ada, b_ada).reshape(DEPTH, B, N_MOD, D)
    rope_m, rope_g = _rope_tables(S)
    lane = np.arange(2 * LANES)
    same_tile = (lane[:, None] // LANES) == (lane[None, :] // LANES)
    same_parity = _GQA_LANE_PARITY[lane % LANES][:, None] == _GQA_LANE_PARITY[lane % LANES][None, :]
    bd_m = jnp.asarray(same_tile, _BF)
    bd_g = jnp.asarray(same_tile & same_parity, _BF)
    w1_in, w1_out = w_ffn1_in.astype(_BF), w_ffn1_out.astype(_BF)
    w2_in, w2_out = w_ffn2_in.astype(_BF), w_ffn2_out.astype(_BF)
    wb_mla, wb_gqa, wo = w_branch_mla.astype(_BF), w_branch_gqa.astype(_BF), w_out.astype(_BF)
    norm_ffn1, norm_mix, norm_ffn2 = (g.reshape(DEPTH, 1, D) for g in (norm_ffn1, norm_mix, norm_ffn2))

    wm = _pack_mix_weights(w_in, mla_q_a_norm, mla_w_uq, mla_kv_a_norm, mla_w_ukv,
                           mla_qk_q_norm, mla_qk_k_norm, gqa_q_norm, gqa_k_norm)

    h = x
    for l in range(DEPTH):
        h = _ffn(h, mod, norm_ffn1, w1_in, w1_out, layer=l, mod0=0)
        q, k, v_t, gate = _mix_in(h, mod, norm_mix, wm, rope_m, rope_g, bd_m, bd_g, layer=l)
        o_att = _attn(q, k, v_t)
        h = _mix_out(h, mod, o_att, gate, wb_mla, wb_gqa, wo, layer=l)
        h = _ffn(h, mod, norm_ffn2, w2_in, w2_out, layer=l, mod0=6)
    return h
```
